```python
import math
import jax
import jax.numpy as jnp
from jax import lax
import numpy as np

D_MODEL = 1024
BATCH = 8
SEQ = 8192
DEPTH = 2

MIX_W = D_MODEL // 2
NSA_HEAD_DIM = 64
NSA_HEADS = MIX_W // NSA_HEAD_DIM
NSA_KV_GROUPS = 2
NSA_HPG = NSA_HEADS // NSA_KV_GROUPS
NSA_KV_W = NSA_KV_GROUPS * NSA_HEAD_DIM
NSA_CMP_BLOCK = 32
NSA_CMP_STRIDE = 16
NSA_CMP_HIDDEN = 256
NSA_SEL_BLOCK = 64
NSA_N_SEL = 16
NSA_WINDOW = 512
Q_BLOCK = 128
NSA_IN = MIX_W + 6 * NSA_KV_W + 3 * NSA_HEADS
RWKV_HEAD_DIM = 64
RWKV_HEADS = MIX_W // RWKV_HEAD_DIM
RWKV_DECAY_RANK = 64
RWKV_AAA_RANK = 64
RWKV_GATE_RANK = 128
RWKV_VRES_RANK = 32
RWKV_LN_EPS = 64e-5
RWKV_IN = 3 * MIX_W + RWKV_DECAY_RANK + RWKV_AAA_RANK + RWKV_GATE_RANK
GDN_HEAD_DIM = 128
GDN_HEADS = MIX_W // GDN_HEAD_DIM
GDN_CONV = 4
GDN_CHUNK = 64
GDN_IN = 4 * MIX_W + 2 * GDN_HEADS
GATE_IN = 3 * D_MODEL
D_IN = NSA_IN + RWKV_IN + GDN_IN + GATE_IN
D_FF = 4 * D_MODEL
EPS = 1e-6
NEG = -1e30
FORCE = 1e4

kernel_name = 'hybrid_nsa_rwkv7_gdn_block'


def _rms(x, g):
    xf = x.astype(jnp.float32)
    y = xf * lax.rsqrt(jnp.mean(xf * xf, axis=-1, keepdims=True) + EPS)
    return (y * g.astype(jnp.float32)).astype(x.dtype)


def _l2n(x):
    xf = x.astype(jnp.float32)
    return (xf * lax.rsqrt(jnp.sum(xf * xf, axis=-1, keepdims=True) + EPS)).astype(x.dtype)


def _shift(x):
    return jnp.pad(x, ((0, 0), (1, 0), (0, 0)))[:, :-1]


def _masked_softmax(s, valid):
    s = jnp.where(valid, s.astype(jnp.float32), NEG)
    p = jax.nn.softmax(s, axis=-1)
    return jnp.where(valid, p, 0.0)


def _nsa(q, kv, gates, q_g, k_g, cmp_pos, cmp_w1, cmp_w2):
    B, S, _ = q.shape
    G, HG, Dh = NSA_KV_GROUPS, NSA_HPG, NSA_HEAD_DIM
    nb = S // Q_BLOCK
    scale = Dh ** -0.5
    qb = _rms(q.reshape(B, S, G, HG, Dh), q_g)
    qb = qb.reshape(B, nb, Q_BLOCK, G, HG, Dh).transpose(1, 0, 3, 4, 2, 5)
    gb = jax.nn.sigmoid(gates.astype(jnp.float32)).reshape(B, nb, Q_BLOCK, G, HG, 3).transpose(1, 0, 3, 4, 2, 5)
    k_c, v_c, k_s, v_s, k_w, v_w = kv.reshape(B, S, 6, G, Dh).transpose(2, 0, 3, 1, 4)

    n_cmp = (S - NSA_CMP_BLOCK) // NSA_CMP_STRIDE + 1
    cmp_start = np.arange(n_cmp) * NSA_CMP_STRIDE
    cmp_idx = cmp_start[:, None] + np.arange(NSA_CMP_BLOCK)[None, :]
    cmp_end = jnp.asarray(cmp_start + NSA_CMP_BLOCK - 1)

    def compress(t, j):
        blk = t[:, :, cmp_idx, :] + cmp_pos[j]
        hid = jax.nn.silu(blk.reshape(B, G, n_cmp, NSA_CMP_BLOCK * Dh) @ cmp_w1[j])
        return hid @ cmp_w2[j]

    kc = _rms(compress(k_c, 0), k_g[0])
    vc = compress(v_c, 1)

    n_slc = S // NSA_SEL_BLOCK
    n_sel = min(NSA_N_SEL, n_slc)
    slc_start = np.arange(n_slc) * NSA_SEL_BLOCK
    overlap = np.clip(np.minimum(cmp_start[:, None] + NSA_CMP_BLOCK, slc_start[None, :] + NSA_SEL_BLOCK)
                      - np.maximum(cmp_start[:, None], slc_start[None, :]), 0, None)
    cmp_to_slc = jnp.asarray(overlap / NSA_CMP_BLOCK, dtype=jnp.float32)
    ks = _rms(k_s, k_g[1]).reshape(B, G, n_slc, NSA_SEL_BLOCK, Dh)
    vs = v_s.reshape(B, G, n_slc, NSA_SEL_BLOCK, Dh)

    pad = ((0, 0), (0, 0), (NSA_WINDOW, 0), (0, 0))
    kw = jnp.pad(_rms(k_w, k_g[2]), pad)
    vw = jnp.pad(v_w, pad)

    bi = jnp.arange(B)[:, None, None]
    gi = jnp.arange(G)[None, :, None]
    slc_ids = jnp.arange(n_slc)
    win_off = jnp.arange(NSA_WINDOW + Q_BLOCK) - NSA_WINDOW
    sel_off = jnp.arange(NSA_SEL_BLOCK)

    def block(args):
        qq, gg, blk = args
        t = blk * Q_BLOCK + jnp.arange(Q_BLOCK)
        s = jnp.einsum('bghqd,bgcd->bghqc', qq, kc) * scale
        p_c = _masked_softmax(s, cmp_end[None, :] <= t[:, None])
        o_c = jnp.einsum('bghqc,bgcd->bghqd', p_c, vc)
        imp = jnp.einsum('bghqc,cj->bgqj', p_c, cmp_to_slc)
        cur = (t // NSA_SEL_BLOCK)[:, None]
        forced = (slc_ids == 0) | (slc_ids == cur) | (slc_ids == cur - 1)
        causal = slc_ids * NSA_SEL_BLOCK <= t[:, None]
        imp = jnp.where(forced, FORCE, jnp.where(causal, imp, -FORCE))
        _, sel = lax.top_k(imp, n_sel)
        flat = sel.reshape(B, G, Q_BLOCK * n_sel)
        ksel = ks[bi, gi, flat].reshape(B, G, Q_BLOCK, n_sel * NSA_SEL_BLOCK, Dh)
        vsel = vs[bi, gi, flat].reshape(B, G, Q_BLOCK, n_sel * NSA_SEL_BLOCK, Dh)
        kpos = (sel[..., None] * NSA_SEL_BLOCK + sel_off).reshape(B, G, Q_BLOCK, n_sel * NSA_SEL_BLOCK)
        s = jnp.einsum('bghqd,bgqkd->bghqk', qq, ksel) * scale
        p_s = _masked_softmax(s, (kpos <= t[:, None])[:, :, None])
        o_s = jnp.einsum('bghqk,bgqkd->bghqd', p_s, vsel)
        start = blk * Q_BLOCK
        kwb = lax.dynamic_slice_in_dim(kw, start, NSA_WINDOW + Q_BLOCK, axis=2)
        vwb = lax.dynamic_slice_in_dim(vw, start, NSA_WINDOW + Q_BLOCK, axis=2)
        kp = start + win_off
        dist = t[:, None] - kp[None, :]
        valid = (dist >= 0) & (dist < NSA_WINDOW) & (kp[None, :] >= 0)
        s = jnp.einsum('bghqd,bgkd->bghqk', qq, kwb) * scale
        p_w = _masked_softmax(s, valid)
        o_w = jnp.einsum('bghqk,bgkd->bghqd', p_w, vwb)
        o = gg[..., 0:1] * o_c + gg[..., 1:2] * o_s + gg[..., 2:3] * o_w
        return o.astype(qq.dtype)

    o = lax.map(block, (qb, gb, jnp.arange(nb)))
    return o.transpose(1, 0, 4, 2, 3, 5).reshape(B, S, MIX_W)


def _rwkv7(cols, xn, v_first, vres, mu, w0, w_up, a0, a_up, g_up, k_k, k_a, r_k, ln_w, ln_b):
    B, S, _ = cols.shape
    H, N = RWKV_HEADS, RWKV_HEAD_DIM
    f32 = jnp.float32
    c = cols.astype(f32)
    c = c + (_shift(c) - c) * mu
    splits = np.cumsum([MIX_W, MIX_W, MIX_W, RWKV_DECAY_RANK, RWKV_AAA_RANK]).tolist()
    r, k, v, wd, ad, gd = jnp.split(c, splits, axis=-1)
    w = -jax.nn.softplus(-(w0 + jnp.tanh(wd) @ w_up)) - 0.5
    decay = jnp.exp(-jnp.exp(w))
    a = jax.nn.sigmoid(a0 + ad @ a_up)
    g = jax.nn.sigmoid(gd) @ g_up
    if vres is None:
        v_first = v
    else:
        v0, vd, vu = vres
        v = v + (v_first - v) * jax.nn.sigmoid(v0 + (xn.astype(f32) @ vd) @ vu)
    heads = lambda t: t.reshape(B, S, H, N)
    kk = _l2n(heads(k * k_k))
    k = k * (1.0 + (a - 1.0) * k_a)
    r, k, v, a, decay = (heads(t) for t in (r, k, v, a, decay))

    def step(state, xs):
        r_t, w_t, k_t, v_t, kk_t, a_t = xs
        sa = jnp.einsum('bhij,bhj->bhi', state, -kk_t)
        state = (state * w_t[:, :, None, :] + sa[..., :, None] * (kk_t * a_t)[..., None, :]
                 + v_t[..., :, None] * k_t[..., None, :])
        return state, jnp.einsum('bhij,bhj->bhi', state, r_t)

    tm = lambda t: jnp.swapaxes(t, 0, 1)
    state0 = jnp.zeros((B, H, N, N), f32)
    _, y = lax.scan(step, state0, (tm(r), tm(decay), tm(k), tm(v), tm(kk), tm(a)))
    y = tm(y)
    mean = jnp.mean(y, axis=-1, keepdims=True)
    var = jnp.mean(jnp.square(y - mean), axis=-1, keepdims=True)
    y = (y - mean) * lax.rsqrt(var + RWKV_LN_EPS) * ln_w.reshape(H, N) + ln_b.reshape(H, N)
    y = y + jnp.sum(r * k * r_k.reshape(H, N), axis=-1, keepdims=True) * v
    y = y.reshape(B, S, MIX_W) * g
    return y.astype(cols.dtype), v_first


def _causal_conv(x, w):
    K, C = w.shape
    return lax.conv_general_dilated(x, w[:, None, :], (1,), [(K - 1, 0)],
                                    dimension_numbers=('NWC', 'WIO', 'NWC'), feature_group_count=C)


def _chunk_gated_delta(q, k, v, g, beta):
    B, H, S, Dk = q.shape
    Dv = v.shape[-1]
    C = GDN_CHUNK
    n = S // C
    q = q.reshape(B, H, n, C, Dk)
    k = k.reshape(B, H, n, C, Dk)
    v = v.reshape(B, H, n, C, Dv)
    g = g.reshape(B, H, n, C)
    beta = beta.reshape(B, H, n, C)
    gam = jnp.cumsum(g, axis=-1)
    idx = jnp.arange(C)
    lower = idx[:, None] >= idx[None, :]
    strict = idx[:, None] > idx[None, :]
    decay = jnp.exp(jnp.where(lower, gam[..., :, None] - gam[..., None, :], -jnp.inf))
    kb = k * beta[..., None]
    a_mat = jnp.eye(C, dtype=q.dtype) + jnp.where(strict, jnp.einsum('bhncd,bhnsd->bhncs', kb, k) * decay, 0.0)
    u = lax.linalg.triangular_solve(a_mat, v * beta[..., None], left_side=True, lower=True, unit_diagonal=True)
    w = lax.linalg.triangular_solve(a_mat, kb * jnp.exp(gam)[..., None], left_side=True, lower=True,
                                    unit_diagonal=True)
    qk = jnp.einsum('bhncd,bhnsd->bhncs', q, k) * decay
    qg = q * jnp.exp(gam)[..., None]
    kg = k * jnp.exp(gam[..., -1:] - gam)[..., None]
    g_last = jnp.exp(gam[..., -1])

    def step(state, xs):
        qg_c, kg_c, u_c, w_c, qk_c, gl_c = xs
        v_new = u_c - jnp.einsum('bhcd,bhdv->bhcv', w_c, state)
        o = jnp.einsum('bhcd,bhdv->bhcv', qg_c, state) + jnp.einsum('bhcs,bhsv->bhcv', qk_c, v_new)
        state = state * gl_c[..., None, None] + jnp.einsum('bhcd,bhcv->bhdv', kg_c, v_new)
        return state, o

    mv = lambda t: jnp.moveaxis(t, 2, 0)
    state0 = jnp.zeros((B, H, Dk, Dv), q.dtype)
    _, o = lax.scan(step, state0, (mv(qg), mv(kg), mv(u), mv(w), mv(qk), mv(g_last)))
    return jnp.moveaxis(o, 0, 2).reshape(B, H, S, Dv)


def _gdn(cols, conv_w, a_log, dt_bias, norm_w):
    B, S, _ = cols.shape
    H, Dh = GDN_HEADS, GDN_HEAD_DIM
    f32 = jnp.float32
    qkv = jax.nn.silu(_causal_conv(cols[..., :3 * MIX_W], conv_w)).astype(f32)
    z = cols[..., 3 * MIX_W:4 * MIX_W].astype(f32)
    a = cols[..., 4 * MIX_W:4 * MIX_W + H].astype(f32)
    b = cols[..., 4 * MIX_W + H:].astype(f32)
    q, k, v = (t.reshape(B, S, H, Dh).transpose(0, 2, 1, 3) for t in jnp.split(qkv, 3, axis=-1))
    q = _l2n(q) * (Dh ** -0.5)
    k = _l2n(k)
    beta = jax.nn.sigmoid(b).transpose(0, 2, 1)
    g = (-jnp.exp(a_log) * jax.nn.softplus(a + dt_bias)).transpose(0, 2, 1)
    o = _chunk_gated_delta(q, k, v, g, beta).transpose(0, 2, 1, 3)
    o = _rms(o, norm_w) * jax.nn.silu(z.reshape(B, S, H, Dh))
    return o.reshape(B, S, MIX_W).astype(cols.dtype)


def setup_inputs(seed: int = 0) -> dict:
    key = jax.random.key(seed)
    keys = iter(jax.random.split(key, 40))
    f32 = jnp.float32

    def nrm(shape, scale):
        return jax.random.normal(next(keys), shape, f32) * scale

    def uni(shape, lo, hi):
        return jax.random.uniform(next(keys), shape, f32, lo, hi)

    L, Lv = DEPTH, DEPTH - 1
    dt = jnp.exp(uni((L, GDN_HEADS), math.log(1e-3), math.log(1e-1)))
    return {
        'x': nrm((BATCH, SEQ, D_MODEL), 1.0),
        'norm_mix_g': 1.0 + nrm((L, D_MODEL), 0.02),
        'w_in': nrm((L, D_MODEL, D_IN), D_MODEL ** -0.5),
        'nsa_q_norm': 1.0 + nrm((L, NSA_HEAD_DIM), 0.02),
        'nsa_k_norm': 1.0 + nrm((L, 3, NSA_HEAD_DIM), 0.02),
        'nsa_cmp_pos': nrm((L, 2, NSA_CMP_BLOCK, NSA_HEAD_DIM), 0.02),
        'nsa_cmp_w1': nrm((L, 2, NSA_CMP_BLOCK * NSA_HEAD_DIM, NSA_CMP_HIDDEN), (NSA_CMP_BLOCK * NSA_HEAD_DIM) ** -0.5),
        'nsa_cmp_w2': nrm((L, 2, NSA_CMP_HIDDEN, NSA_HEAD_DIM), NSA_CMP_HIDDEN ** -0.5),
        'rwkv_mu': uni((L, RWKV_IN), 0.0, 1.0),
        'rwkv_w0': uni((L, MIX_W), -6.0, -1.0),
        'rwkv_w_up': nrm((L, RWKV_DECAY_RANK, MIX_W), 0.1 * RWKV_DECAY_RANK ** -0.5),
        'rwkv_a0': nrm((L, MIX_W), 0.1),
        'rwkv_a_up': nrm((L, RWKV_AAA_RANK, MIX_W), 0.1 * RWKV_AAA_RANK ** -0.5),
        'rwkv_g_up': nrm((L, RWKV_GATE_RANK, MIX_W), RWKV_GATE_RANK ** -0.5),
        'rwkv_k_k': 0.85 + nrm((L, MIX_W), 0.02),
        'rwkv_k_a': 1.0 + nrm((L, MIX_W), 0.02),
        'rwkv_r_k': nrm((L, MIX_W), 0.1),
        'rwkv_ln_w': 1.0 + nrm((L, MIX_W), 0.02),
        'rwkv_ln_b': nrm((L, MIX_W), 0.02),
        'rwkv_v0': nrm((Lv, MIX_W), 0.1),
        'rwkv_vres_down': nrm((Lv, D_MODEL, RWKV_VRES_RANK), D_MODEL ** -0.5),
        'rwkv_vres_up': nrm((Lv, RWKV_VRES_RANK, MIX_W), 0.1 * RWKV_VRES_RANK ** -0.5),
        'gdn_conv_w': nrm((L, GDN_CONV, 3 * MIX_W), GDN_CONV ** -0.5),
        'gdn_a_log': jnp.log(uni((L, GDN_HEADS), 1.0, 16.0)),
        'gdn_dt_bias': dt + jnp.log(-jnp.expm1(-dt)),
        'gdn_norm_w': 1.0 + nrm((L, GDN_HEAD_DIM), 0.02),
        'w_branch': nrm((L, 3, MIX_W, D_MODEL), MIX_W ** -0.5),
        'w_out': nrm((L, D_MODEL, D_MODEL), D_MODEL ** -0.5),
        'norm_ffn_g': 1.0 + nrm((L, D_MODEL), 0.02),
        'w_ff1': nrm((L, D_MODEL, D_FF), D_MODEL ** -0.5),
        'w_ff2': nrm((L, D_FF, D_MODEL), D_FF ** -0.5),
    }


def reference(x, norm_mix_g, w_in, nsa_q_norm, nsa_k_norm, nsa_cmp_pos, nsa_cmp_w1, nsa_cmp_w2,
              rwkv_mu, rwkv_w0, rwkv_w_up, rwkv_a0, rwkv_a_up, rwkv_g_up, rwkv_k_k, rwkv_k_a, rwkv_r_k,
              rwkv_ln_w, rwkv_ln_b, rwkv_v0, rwkv_vres_down, rwkv_vres_up,
              gdn_conv_w, gdn_a_log, gdn_dt_bias, gdn_norm_w,
              w_branch, w_out, norm_ffn_g, w_ff1, w_ff2):
    B, S, D = x.shape
    v_first = None
    for i in range(DEPTH):
        h = _rms(x, norm_mix_g[i])
        proj = h @ w_in[i]
        p_nsa, p_rwkv, p_gdn, p_gate = jnp.split(
            proj, [NSA_IN, NSA_IN + RWKV_IN, NSA_IN + RWKV_IN + GDN_IN], axis=-1)
        o_a = _nsa(p_nsa[..., :MIX_W], p_nsa[..., MIX_W:MIX_W + 6 * NSA_KV_W], p_nsa[..., MIX_W + 6 * NSA_KV_W:],
                   nsa_q_norm[i], nsa_k_norm[i], nsa_cmp_pos[i], nsa_cmp_w1[i], nsa_cmp_w2[i])
        vres = None if i == 0 else (rwkv_v0[i - 1], rwkv_vres_down[i - 1], rwkv_vres_up[i - 1])
        o_b, v_first = _rwkv7(p_rwkv, h, v_first, vres, rwkv_mu[i], rwkv_w0[i], rwkv_w_up[i], rwkv_a0[i],
                              rwkv_a_up[i], rwkv_g_up[i], rwkv_k_k[i], rwkv_k_a[i], rwkv_r_k[i],
                              rwkv_ln_w[i], rwkv_ln_b[i])
        o_c = _gdn(p_gdn, gdn_conv_w[i], gdn_a_log[i], gdn_dt_bias[i], gdn_norm_w[i])
        gate = jax.nn.sigmoid(p_gate.reshape(B, S, 3, D))
        merged = (gate[:, :, 0] * (o_a @ w_branch[i, 0]) + gate[:, :, 1] * (o_b @ w_branch[i, 1])
                  + gate[:, :, 2] * (o_c @ w_branch[i, 2]))
        x = x + merged @ w_out[i]
        h = _rms(x, norm_ffn_g[i])
        x = x + jnp.square(jax.nn.relu(h @ w_ff1[i])) @ w_ff2[i]
    return x
```

```python
import functools
import math

import numpy as np
import jax
import jax.numpy as jnp
from jax import lax
from jax.experimental import pallas as pl
from jax.experimental.pallas import tpu as pltpu

F32 = jnp.float32
BF16 = jnp.bfloat16

D_MODEL = 1024
MIX_W = 512
NSA_HEAD_DIM = 64
NSA_KV_GROUPS = 2
NSA_HPG = 4
NSA_CMP_BLOCK = 32
NSA_CMP_STRIDE = 16
NSA_CMP_HIDDEN = 256
NSA_SEL_BLOCK = 64
NSA_N_SEL = 16
NSA_WINDOW = 512
Q_BLOCK = 128
NSA_IN = 1304
RWKV_HEADS = 8
RWKV_HEAD_DIM = 64
RWKV_LN_EPS = 64e-5
RWKV_IN = 1792
GDN_HEADS = 4
GDN_HEAD_DIM = 128
GDN_CONV = 4
GDN_IN = 2056
D_IN = 8224
D_FF = 4096
EPS = 1e-6
NEG = -1e30
FORCE = 1e4

C_KV = 0
C_Q = 768
C_SM0 = 1280
C_SM1 = 1408
C_RKV = 1536
C_GDN = 3072
C_GATE = 5120
C_LR = 8192
N_PROJ = 8448
VMEM_LIMIT = 48 * 1024 * 1024


def _proj_perm():
    zero = D_IN + 32
    perm = np.full((N_PROJ,), zero, np.int32)
    perm[C_KV:C_KV + 768] = np.arange(512, 1280)
    perm[C_Q:C_Q + 512] = np.arange(0, 512)
    perm[C_SM0:C_SM0 + 12] = 1280 + np.arange(12)
    perm[C_SM1:C_SM1 + 12] = 1292 + np.arange(12)
    rw = NSA_IN
    gd = NSA_IN + RWKV_IN
    gt = gd + GDN_IN
    perm[C_SM0 + 16:C_SM0 + 24] = gd + 2048 + np.arange(8)
    perm[C_SM0 + 32:C_SM0 + 64] = D_IN + np.arange(32)
    perm[C_RKV:C_RKV + 1536] = rw + np.arange(1536)
    perm[C_LR:C_LR + 256] = rw + 1536 + np.arange(256)
    perm[C_GDN:C_GDN + 2048] = gd + np.arange(2048)
    perm[C_GATE:C_GATE + 3072] = gt + np.arange(3072)
    return perm


def _cp(*sem):
    return pltpu.CompilerParams(dimension_semantics=sem, vmem_limit_bytes=VMEM_LIMIT)


def _dot(a, b):
    return jnp.dot(a, b, preferred_element_type=F32)


def _dot_nt(a, b):
    return lax.dot_general(a, b, (((1,), (1,)), ((), ())), preferred_element_type=F32)


def _dot_tn(a, b):
    return lax.dot_general(a, b, (((0,), (0,)), ((), ())), preferred_element_type=F32)


def _split(a):
    hi = a.astype(BF16)
    lo = (a - hi.astype(F32)).astype(BF16)
    return hi, lo


def _mm(a, b, fn=_dot, passes=1):
    if passes == 1:
        return fn(a.astype(BF16), b.astype(BF16))
    ah, al = _split(a)
    bh, bl = _split(b)
    return fn(ah, bh) + (fn(ah, bl) + fn(al, bh))


def _cumsum_rows(ltri, x):
    hi = x.astype(BF16)
    r1 = x - hi.astype(F32)
    mid = r1.astype(BF16)
    lo = (r1 - mid.astype(F32)).astype(BF16)
    return _dot(ltri, hi) + (_dot(ltri, mid) + _dot(ltri, lo))


def _tri_masks(c):
    row = lax.broadcasted_iota(jnp.int32, (c, c), 0)
    col = lax.broadcasted_iota(jnp.int32, (c, c), 1)
    return row >= col, row > col


def _unit_lower_inverse(a, passes):
    c = a.shape[0]
    row = lax.broadcasted_iota(jnp.int32, (c, c), 0)
    col = lax.broadcasted_iota(jnp.int32, (c, c), 1)
    p = -a
    t = jnp.where(row == col, 1.0, 0.0).astype(F32) + p
    for _ in range(int(math.log2(c)) - 1):
        p = _mm(p, p, passes=passes)
        t = t + _mm(t, p, passes=passes)
    return t


def _softplus(z):
    return jnp.maximum(z, 0.0) + jnp.log1p(jnp.exp(-jnp.abs(z)))


def _sigmoid(z):
    return 1.0 / (1.0 + jnp.exp(-z))


def _rms_matmul_kernel(x_ref, g_ref, w_ref, o_ref, hn_ref):
    @pl.when(pl.program_id(1) == 0)
    def _():
        x = x_ref[...]
        ms = jnp.mean(x * x, axis=-1, keepdims=True)
        hn_ref[...] = (x * lax.rsqrt(ms + EPS) * g_ref[...]).astype(BF16)

    o_ref[...] = _dot(hn_ref[...], w_ref[...])


def _rms_matmul(x2, g, w, tm=512, tn=768):
    t, d = x2.shape
    n = w.shape[1]
    tm = min(tm, t)
    return pl.pallas_call(
        _rms_matmul_kernel,
        grid=(t // tm, n // tn),
        in_specs=[pl.BlockSpec((tm, d), lambda i, j: (i, 0)),
                  pl.BlockSpec((1, d), lambda i, j: (0, 0)),
                  pl.BlockSpec((d, tn), lambda i, j: (0, j))],
        out_specs=pl.BlockSpec((tm, tn), lambda i, j: (i, j)),
        out_shape=jax.ShapeDtypeStruct((t, n), F32),
        scratch_shapes=[pltpu.VMEM((tm, d), BF16)],
        compiler_params=_cp("parallel", "arbitrary"),
        name="rms_proj",
    )(x2, g.reshape(1, d), w)


def _nsa_kvprep_kernel(kv_ref, kg_ref, kc_ref, vc_ref, ks_ref, vs_ref, kw_ref, vw_ref):
    outs = (kc_ref, vc_ref, ks_ref, vs_ref, kw_ref, vw_ref)
    for j in range(6):
        for g in range(NSA_KV_GROUPS):
            lo = j * 128 + g * NSA_HEAD_DIM
            piece = kv_ref[:, lo:lo + NSA_HEAD_DIM]
            if j in (2, 4):
                gain = kg_ref[j // 2:j // 2 + 1, :]
                ms = jnp.mean(piece * piece, axis=-1, keepdims=True)
                piece = piece * lax.rsqrt(ms + EPS) * gain
            outs[j][g] = piece.astype(outs[j].dtype)


def _nsa_kvprep(proj3, k_g, ts=1024):
    b, s, _ = proj3.shape
    g, dh = NSA_KV_GROUPS, NSA_HEAD_DIM
    out_spec = pl.BlockSpec((None, g, ts, dh), lambda bi, i: (bi, 0, i, 0))
    shp = lambda dt: jax.ShapeDtypeStruct((b, g, s, dh), dt)
    return pl.pallas_call(
        _nsa_kvprep_kernel,
        grid=(b, s // ts),
        in_specs=[pl.BlockSpec((None, ts, 768), lambda bi, i: (bi, i, C_KV // 768)),
                  pl.BlockSpec((3, dh), lambda bi, i: (0, 0))],
        out_specs=[out_spec] * 6,
        out_shape=[shp(F32), shp(F32), shp(BF16), shp(BF16), shp(BF16), shp(BF16)],
        compiler_params=_cp("parallel", "parallel"),
        name="nsa_kvprep",
    )(proj3, k_g)


def _nsa_compress_kernel(x_ref, w1_ref, pos_ref, w2_ref, g_ref, o_ref, *, normalize):
    x = x_ref[...].astype(BF16)
    half = x.shape[1]
    a = _dot(x, w1_ref[0:half, :])
    bm = _dot(x, w1_ref[half:2 * half, :])
    bias = _dot(pos_ref[...].astype(BF16), w1_ref[...])[0:1, :]
    n = x.shape[0]
    hid = a + pltpu.roll(bm, n - 1, 0) + bias
    hid = hid * _sigmoid(hid)
    out = _dot(hid.astype(BF16), w2_ref[...])
    if normalize:
        ms = jnp.mean(out * out, axis=-1, keepdims=True)
        out = out * lax.rsqrt(ms + EPS) * g_ref[...]
    o_ref[...] = out.astype(o_ref.dtype)


def _nsa_compress(xh, w1, pos, w2, gain, normalize):
    bg, n, half = xh.shape
    dh = NSA_HEAD_DIM
    pos8 = jnp.broadcast_to(pos.reshape(1, 2 * half), (8, 2 * half))
    return pl.pallas_call(
        functools.partial(_nsa_compress_kernel, normalize=normalize),
        grid=(bg,),
        in_specs=[pl.BlockSpec((None, n, half), lambda i: (i, 0, 0)),
                  pl.BlockSpec((2 * half, NSA_CMP_HIDDEN), lambda i: (0, 0)),
                  pl.BlockSpec((8, 2 * half), lambda i: (0, 0)),
                  pl.BlockSpec((NSA_CMP_HIDDEN, dh), lambda i: (0, 0)),
                  pl.BlockSpec((1, dh), lambda i: (0, 0))],
        out_specs=pl.BlockSpec((None, n, dh), lambda i: (i, 0, 0)),
        out_shape=jax.ShapeDtypeStruct((bg, n, dh), BF16),
        compiler_params=_cp("parallel"),
        name="nsa_compress",
    )(xh, w1.astype(BF16), pos8, w2.astype(BF16), gain.reshape(1, dh))


def _nsa_attn_kernel(q_ref, sm_ref, kc_ref, vc_ref, ks_ref, vs_ref, kw_ref, vw_ref, e_ref, c2s_ref, qg_ref, o_ref,
                     *, n_sel, tk):
    dh, hg, qb = NSA_HEAD_DIM, NSA_HPG, Q_BLOCK
    rows = hg * qb
    blk = pl.program_id(2)
    start = blk * qb
    scale = dh ** -0.5

    xq = q_ref[...]
    qs = []
    for h in range(hg):
        qh = xq[:, h * dh:(h + 1) * dh]
        ms = jnp.mean(qh * qh, axis=-1, keepdims=True)
        qs.append(qh * lax.rsqrt(ms + EPS) * (qg_ref[...] * scale))
    q = jnp.concatenate(qs, axis=0).astype(BF16)
    tq = start + lax.broadcasted_iota(jnp.int32, (qb, 1), 0)
    t_row = jnp.concatenate([tq] * hg, axis=0)

    n_cmp = kc_ref.shape[0]
    s = _dot_nt(q, kc_ref[...])
    cend = lax.broadcasted_iota(jnp.int32, (1, n_cmp), 1) * NSA_CMP_STRIDE + (NSA_CMP_BLOCK - 1)
    valid = cend <= t_row
    s = jnp.where(valid, s, NEG)
    m = jnp.max(s, axis=-1, keepdims=True)
    p = jnp.where(valid, jnp.exp(s - m), 0.0)
    l = jnp.sum(p, axis=-1, keepdims=True)
    p = p * (1.0 / jnp.maximum(l, 1e-30))
    o_c = _dot(p.astype(BF16), vc_ref[...])
    psum = p[0:qb]
    for h in range(1, hg):
        psum = psum + p[h * qb:(h + 1) * qb]
    c2s = c2s_ref[...]
    ph = psum.astype(BF16)
    r1 = psum - ph.astype(F32)
    pm = r1.astype(BF16)
    plo = (r1 - pm.astype(F32)).astype(BF16)
    imp = _dot(ph, c2s) + (_dot(pm, c2s) + _dot(plo, c2s))

    n_slc = imp.shape[1]
    jf = lax.broadcasted_iota(jnp.int32, (1, n_slc), 1)
    cur = jnp.right_shift(tq, int(math.log2(NSA_SEL_BLOCK)))
    forced = (jf == 0) | (jf == cur) | (jf == cur - 1)
    causal = jf * NSA_SEL_BLOCK <= tq
    val = jnp.where(forced, FORCE, jnp.where(causal, imp, -FORCE))
    jff = jf.astype(F32)
    sel = jnp.zeros((qb, n_slc), F32)
    for _ in range(n_sel):
        mx = jnp.max(val, axis=-1, keepdims=True)
        idx = jnp.min(jnp.where(val == mx, jff, float(n_slc)), axis=-1, keepdims=True)
        hit = jff == idx
        sel = jnp.where(hit, 1.0, sel)
        val = jnp.where(hit, -jnp.inf, val)
    selb = sel.astype(BF16)

    def sel_body(kt, carry):
        m_i, l_i, acc = carry
        off = pl.multiple_of(kt * tk, tk)
        k = ks_ref[pl.ds(off, tk), :]
        v = vs_ref[pl.ds(off, tk), :]
        s = _dot_nt(q, k)
        member = _dot(selb, e_ref[:, pl.ds(off, tk)])
        kpos = off + lax.broadcasted_iota(jnp.int32, (1, tk), 1)
        bias1 = jnp.where((member > 0.5) & (kpos <= tq), 0.0, NEG)
        s = s + jnp.concatenate([bias1] * hg, axis=0)
        m_new = jnp.maximum(m_i, jnp.max(s, axis=-1, keepdims=True))
        alpha = jnp.exp(m_i - m_new)
        p = jnp.exp(s - m_new)
        l_new = alpha * l_i + jnp.sum(p, axis=-1, keepdims=True)
        acc = alpha * acc + _dot(p.astype(BF16), v)
        return m_new, l_new, acc

    n_kt = (start + qb + tk - 1) // tk
    init = (jnp.full((rows, 1), NEG, F32), jnp.zeros((rows, 1), F32), jnp.zeros((rows, dh), F32))
    _, l_s, acc_s = lax.fori_loop(0, n_kt, sel_body, init)
    o_s = acc_s * (1.0 / l_s)

    wlen = NSA_WINDOW + qb
    base = pl.multiple_of(jnp.maximum(start - NSA_WINDOW, 0), qb)
    k = kw_ref[pl.ds(base, wlen), :]
    v = vw_ref[pl.ds(base, wlen), :]
    s = _dot_nt(q, k)
    kp = base + lax.broadcasted_iota(jnp.int32, (1, wlen), 1)
    dist = t_row - kp
    valid = (dist >= 0) & (dist < NSA_WINDOW)
    s = jnp.where(valid, s, NEG)
    m = jnp.max(s, axis=-1, keepdims=True)
    p = jnp.where(valid, jnp.exp(s - m), 0.0)
    l = jnp.sum(p, axis=-1, keepdims=True)
    o_w = _dot(p.astype(BF16), v) * (1.0 / l)

    gs = _sigmoid(sm_ref[...])
    for h in range(hg):
        r = slice(h * qb, (h + 1) * qb)
        o = (gs[:, 3 * h:3 * h + 1] * o_c[r] + gs[:, 3 * h + 1:3 * h + 2] * o_s[r]
             + gs[:, 3 * h + 2:3 * h + 3] * o_w[r])
        o_ref[:, h * dh:(h + 1) * dh] = o.astype(o_ref.dtype)


def _nsa(proj3, q_g, k_g, cmp_pos, cmp_w1, cmp_w2):
    b, s, _ = proj3.shape
    g, dh, qb = NSA_KV_GROUPS, NSA_HEAD_DIM, Q_BLOCK
    kc_raw, vc_raw, ks, vs, kw, vw = _nsa_kvprep(proj3, k_g, ts=min(1024, s))
    n_half = s // NSA_CMP_STRIDE
    half = NSA_CMP_STRIDE * dh
    kc = _nsa_compress(kc_raw.reshape(b * g, n_half, half), cmp_w1[0], cmp_pos[0], cmp_w2[0], k_g[0], True)
    vc = _nsa_compress(vc_raw.reshape(b * g, n_half, half), cmp_w1[1], cmp_pos[1], cmp_w2[1], k_g[0], False)
    kc = kc.reshape(b, g, n_half, dh)
    vc = vc.reshape(b, g, n_half, dh)

    n_slc = s // NSA_SEL_BLOCK
    n_sel = min(NSA_N_SEL, n_slc)
    tk = 512
    e = (np.arange(s)[None, :] // NSA_SEL_BLOCK == np.arange(n_slc)[:, None]).astype(np.float32)
    cmp_start = np.arange(n_half) * NSA_CMP_STRIDE
    slc_start = np.arange(n_slc) * NSA_SEL_BLOCK
    overlap = np.clip(np.minimum(cmp_start[:, None] + NSA_CMP_BLOCK, slc_start[None, :] + NSA_SEL_BLOCK)
                      - np.maximum(cmp_start[:, None], slc_start[None, :]), 0, None)
    c2s = overlap / NSA_CMP_BLOCK

    kvc_spec = pl.BlockSpec((None, None, n_half, dh), lambda bi, gi, i: (bi, gi, 0, 0))
    kv_spec = pl.BlockSpec((None, None, s, dh), lambda bi, gi, i: (bi, gi, 0, 0))
    return pl.pallas_call(
        functools.partial(_nsa_attn_kernel, n_sel=n_sel, tk=tk),
        grid=(b, g, s // qb),
        in_specs=[pl.BlockSpec((None, qb, 256), lambda bi, gi, i: (bi, i, C_Q // 256 + gi)),
                  pl.BlockSpec((None, qb, 128), lambda bi, gi, i: (bi, i, C_SM0 // 128 + gi)),
                  kvc_spec, kvc_spec, kv_spec, kv_spec, kv_spec, kv_spec,
                  pl.BlockSpec((n_slc, s), lambda bi, gi, i: (0, 0)),
                  pl.BlockSpec((n_half, n_slc), lambda bi, gi, i: (0, 0)),
                  pl.BlockSpec((1, dh), lambda bi, gi, i: (0, 0))],
        out_specs=pl.BlockSpec((None, qb, 256), lambda bi, gi, i: (bi, i, gi)),
        out_shape=jax.ShapeDtypeStruct((b, s, MIX_W), BF16),
        compiler_params=_cp("parallel", "parallel", "arbitrary"),
        name="nsa_attn",
    )(proj3, proj3, kc, vc, ks, vs, kw, vw, jnp.asarray(e, BF16), jnp.asarray(c2s, BF16), q_g.reshape(1, dh))


def _shift_rows(x, prev_row, first):
    rolled = pltpu.roll(x, 1, 0)
    row = lax.broadcasted_iota(jnp.int32, x.shape, 0)
    prev = jnp.where(first, 0.0, prev_row)
    return jnp.where(row == 0, prev, rolled)


def _rwkv_prep_kernel(*refs, has_vres):
    if has_vres:
        (rkv_ref, rkvp_ref, lr_ref, lrp_ref, mu1_ref, mu2_ref, w0_ref, wup_ref, a0_ref, aup_ref, gup_ref,
         sm_ref, vf_ref, v0_ref, vu_ref, r_ref, k_ref, v_ref, a_ref, lw_ref, g_ref) = refs
    else:
        (rkv_ref, rkvp_ref, lr_ref, lrp_ref, mu1_ref, mu2_ref, w0_ref, wup_ref, a0_ref, aup_ref, gup_ref,
         r_ref, k_ref, v_ref, a_ref, lw_ref, g_ref) = refs
    first = pl.program_id(1) == 0
    w = MIX_W
    c = rkv_ref[...]
    c = c + (_shift_rows(c, rkvp_ref[7:8, :], first) - c) * mu1_ref[...]
    lr = lr_ref[...]
    lr = lr + (_shift_rows(lr, lrp_ref[7:8, :], first) - lr) * mu2_ref[...]
    r, k, v = c[:, 0:w], c[:, w:2 * w], c[:, 2 * w:3 * w]
    wd, ad, gd = lr[:, 0:64], lr[:, 64:128], lr[:, 128:256]
    wlog = -_softplus(-(w0_ref[...] + _mm(jnp.tanh(wd), wup_ref[...], passes=3))) - 0.5
    lw_ref[...] = -jnp.exp(wlog)
    a = _sigmoid(a0_ref[...] + _mm(ad, aup_ref[...], passes=3))
    g_ref[...] = _mm(_sigmoid(gd), gup_ref[...], passes=3)
    if has_vres:
        mix = _sigmoid(v0_ref[...] + _mm(sm_ref[:, 32:64], vu_ref[...], passes=3))
        v = v + (vf_ref[...] - v) * mix
    r_ref[...] = r
    k_ref[...] = k
    v_ref[...] = v
    a_ref[...] = a


def _rwkv_prep(proj3, mu, w0, w_up, a0, a_up, g_up, v_first, vres, ts=512):
    b, s, _ = proj3.shape
    w = MIX_W
    has_vres = vres is not None
    ts = min(ts, s)
    nrb = ts // 8
    cur = lambda wd, cb: pl.BlockSpec((None, ts, wd), lambda bi, i: (bi, i, cb))
    prev = lambda wd, cb: pl.BlockSpec((None, 8, wd), lambda bi, i: (bi, jnp.maximum(i * nrb - 1, 0), cb))
    full = lambda shp: pl.BlockSpec(shp, lambda bi, i: (0,) * len(shp))
    in_specs = [cur(1536, C_RKV // 1536), prev(1536, C_RKV // 1536), cur(256, C_LR // 256), prev(256, C_LR // 256),
                full((1, 1536)), full((1, 256)), full((1, w)), full((64, w)), full((1, w)), full((64, w)),
                full((128, w))]
    args = [proj3, proj3, proj3, proj3, mu[:1536].reshape(1, 1536), mu[1536:].reshape(1, 256), w0.reshape(1, w),
            w_up, a0.reshape(1, w), a_up, g_up]
    if has_vres:
        v0, vu = vres
        in_specs += [cur(128, C_SM0 // 128), pl.BlockSpec((None, ts, w), lambda bi, i: (bi, i, 0)),
                     full((1, w)), full((32, w))]
        args += [proj3, v_first, v0.reshape(1, w), vu]
    out_spec = pl.BlockSpec((None, ts, w), lambda bi, i: (bi, i, 0))
    shp = jax.ShapeDtypeStruct((b, s, w), F32)
    return pl.pallas_call(
        functools.partial(_rwkv_prep_kernel, has_vres=has_vres),
        grid=(b, s // ts),
        in_specs=in_specs,
        out_specs=[out_spec] * 6,
        out_shape=[shp] * 6,
        compiler_params=_cp("parallel", "parallel"),
        name="rwkv_prep",
    )(*args)


RWKV_CHUNK = 64


def _rwkv_chunk_kernel(r_ref, k_ref, v_ref, a_ref, lw_ref, g_ref, kk_ref, ka_ref, rk_ref, lnw_ref, lnb_ref,
                       o_ref, s_ref, *, passes):
    c, n, nh = RWKV_CHUNK, RWKV_HEAD_DIM, RWKV_HEADS

    @pl.when(pl.program_id(1) == 0)
    def _():
        s_ref[...] = jnp.zeros_like(s_ref)

    incl, strict = _tri_masks(c)
    ltri = jnp.where(incl, 1.0, 0.0).astype(BF16)
    mm = functools.partial(_mm, passes=passes)

    def chunk(ci, carry):
        rows = pl.ds(pl.multiple_of(ci * c, c), c)
        for h in range(nh):
            ls = slice(h * n, (h + 1) * n)
            r = r_ref[rows, ls]
            k = k_ref[rows, ls]
            v = v_ref[rows, ls]
            a = a_ref[rows, ls]
            lw = lw_ref[rows, ls]
            kk = k * kk_ref[:, ls]
            kk = kk * lax.rsqrt(jnp.sum(kk * kk, axis=-1, keepdims=True) + EPS)
            k = k * (1.0 + (a - 1.0) * ka_ref[:, ls])
            kka = kk * a
            lcum = _cumsum_rows(ltri, lw)
            lend = lcum[c - 1:c, :]
            e_inv = jnp.exp(-lcum)
            kap = kk * jnp.exp(lcum - lw)
            bet = kka * e_inv
            kt = k * e_inv
            rt = r * jnp.exp(lcum)
            a_kb = jnp.where(strict, mm(kap, bet, _dot_nt), 0.0)
            a_kk = jnp.where(strict, mm(kap, kt, _dot_nt), 0.0)
            a_rk = jnp.where(incl, mm(rt, kt, _dot_nt), 0.0)
            a_rb = jnp.where(incl, mm(rt, bet, _dot_nt), 0.0)
            tinv = _unit_lower_inverse(a_kb, passes)
            st = s_ref[h]
            u = mm(tinv, mm(kap, st, _dot_nt) + mm(a_kk, v))
            y = mm(rt, st, _dot_nt) + mm(a_rk, v) - mm(a_rb, u)
            e_end = jnp.exp(lend - lcum)
            s_ref[h] = st * jnp.exp(lend) + mm(v, k * e_end, _dot_tn) - mm(u, kka * e_end, _dot_tn)
            mean = jnp.mean(y, axis=-1, keepdims=True)
            yc = y - mean
            var = jnp.mean(yc * yc, axis=-1, keepdims=True)
            y = yc * lax.rsqrt(var + RWKV_LN_EPS) * lnw_ref[:, ls] + lnb_ref[:, ls]
            y = y + jnp.sum(r * k * rk_ref[:, ls], axis=-1, keepdims=True) * v
            o_ref[rows, ls] = (y * g_ref[rows, ls]).astype(o_ref.dtype)
        return carry

    lax.fori_loop(0, r_ref.shape[0] // c, chunk, 0)


def _rwkv_chunk(r, k, v, a, lw, g, k_k, k_a, r_k, ln_w, ln_b, lblk=512, passes=3):
    b, s, w = r.shape
    lblk = min(lblk, s)
    seq = pl.BlockSpec((None, lblk, w), lambda bi, i: (bi, i, 0))
    par = pl.BlockSpec((1, w), lambda bi, i: (0, 0))
    return pl.pallas_call(
        functools.partial(_rwkv_chunk_kernel, passes=passes),
        grid=(b, s // lblk),
        in_specs=[seq] * 6 + [par] * 5,
        out_specs=seq,
        out_shape=jax.ShapeDtypeStruct((b, s, w), BF16),
        scratch_shapes=[pltpu.VMEM((RWKV_HEADS, RWKV_HEAD_DIM, RWKV_HEAD_DIM), F32)],
        compiler_params=_cp("parallel", "arbitrary"),
        name="rwkv_chunk",
    )(r, k, v, a, lw, g, *(p.reshape(1, w) for p in (k_k, k_a, r_k, ln_w, ln_b)))


GDN_CHUNK = 64


def _gdn_prep_kernel(x_ref, xp_ref, sm_ref, cw_ref, alog_ref, dtb_ref, q_ref, k_ref, kb_ref, vb_ref, gb_ref):
    first = pl.program_id(1) == 0
    dh = GDN_HEAD_DIM
    w = MIX_W
    x = x_ref[...]
    ts = x.shape[0]
    prev = jnp.where(first, 0.0, xp_ref[...])
    row = lax.broadcasted_iota(jnp.int32, x.shape, 0)
    acc = x * cw_ref[GDN_CONV - 1:GDN_CONV, :]
    for d in range(1, GDN_CONV):
        sh = pltpu.roll(x, d, 0)
        for rr in range(d):
            sh = jnp.where(row == rr, prev[8 - d + rr:8 - d + rr + 1, :], sh)
        acc = acc + sh * cw_ref[GDN_CONV - 1 - d:GDN_CONV - d, :]
    act = acc * _sigmoid(acc)
    sm = sm_ref[...]
    for h in range(GDN_HEADS):
        ls = slice(h * dh, (h + 1) * dh)
        q = act[:, h * dh:(h + 1) * dh]
        k = act[:, w + h * dh:w + (h + 1) * dh]
        v = act[:, 2 * w + h * dh:2 * w + (h + 1) * dh]
        q = q * lax.rsqrt(jnp.sum(q * q, axis=-1, keepdims=True) + EPS) * (dh ** -0.5)
        k = k * lax.rsqrt(jnp.sum(k * k, axis=-1, keepdims=True) + EPS)
        beta = _sigmoid(sm[:, 20 + h:21 + h])
        g = -jnp.exp(alog_ref[:, h:h + 1]) * _softplus(sm[:, 16 + h:17 + h] + dtb_ref[:, h:h + 1])
        q_ref[:, ls] = q
        k_ref[:, ls] = k
        kb_ref[:, ls] = k * beta
        vb_ref[:, ls] = v * beta
        gb_ref[:, ls] = jnp.broadcast_to(g, (ts, dh))


def _gdn_prep(proj3, conv_w, a_log, dt_bias, ts=512):
    b, s, _ = proj3.shape
    w = MIX_W
    ts = min(ts, s)
    nrb = ts // 8
    out_spec = pl.BlockSpec((None, ts, w), lambda bi, i: (bi, i, 0))
    shp = jax.ShapeDtypeStruct((b, s, w), F32)
    return pl.pallas_call(
        _gdn_prep_kernel,
        grid=(b, s // ts),
        in_specs=[pl.BlockSpec((None, ts, 3 * w), lambda bi, i: (bi, i, C_GDN // (3 * w))),
                  pl.BlockSpec((None, 8, 3 * w), lambda bi, i: (bi, jnp.maximum(i * nrb - 1, 0), C_GDN // (3 * w))),
                  pl.BlockSpec((None, ts, 128), lambda bi, i: (bi, i, C_SM0 // 128)),
                  pl.BlockSpec((GDN_CONV, 3 * w), lambda bi, i: (0, 0)),
                  pl.BlockSpec((1, GDN_HEADS), lambda bi, i: (0, 0)),
                  pl.BlockSpec((1, GDN_HEADS), lambda bi, i: (0, 0))],
        out_specs=[out_spec] * 5,
        out_shape=[shp] * 5,
        compiler_params=_cp("parallel", "parallel"),
        name="gdn_prep",
    )(proj3, proj3, proj3, conv_w, a_log.reshape(1, GDN_HEADS), dt_bias.reshape(1, GDN_HEADS))


def _gdn_chunk_kernel(q_ref, k_ref, kb_ref, vb_ref, gb_ref, z_ref, nw_ref, o_ref, s_ref, *, passes):
    c = GDN_CHUNK

    @pl.when(pl.program_id(2) == 0)
    def _():
        s_ref[...] = jnp.zeros_like(s_ref)

    incl, strict = _tri_masks(c)
    ltri = jnp.where(incl, 1.0, 0.0).astype(BF16)
    mm = functools.partial(_mm, passes=passes)

    def chunk(ci, carry):
        rows = pl.ds(pl.multiple_of(ci * c, c), c)
        q = q_ref[rows, :]
        k = k_ref[rows, :]
        kb = kb_ref[rows, :]
        vb = vb_ref[rows, :]
        gam = _cumsum_rows(ltri, gb_ref[rows, :])
        gam_r = jnp.transpose(gam)[0:c, :]
        gam_c = gam[:, 0:c]
        decay = jnp.exp(jnp.where(incl, gam_c - gam_r, NEG))
        a_mat = jnp.where(strict, mm(kb, k, _dot_nt) * decay, 0.0)
        tinv = _unit_lower_inverse(a_mat, passes)
        eg = jnp.exp(gam)
        glast = gam[c - 1:c, :]
        u = mm(tinv, vb)
        wm = mm(tinv, kb * eg)
        qk = mm(q, k, _dot_nt) * decay
        st = s_ref[...]
        v_new = u - mm(wm, st)
        o = mm(q * eg, st) + mm(qk, v_new)
        s_ref[...] = st * jnp.exp(glast) + mm(k * jnp.exp(glast - gam), v_new, _dot_tn)
        ms = jnp.mean(o * o, axis=-1, keepdims=True)
        z = z_ref[rows, :]
        o_ref[rows, :] = (o * lax.rsqrt(ms + EPS) * nw_ref[...] * (z * _sigmoid(z))).astype(o_ref.dtype)
        return carry

    lax.fori_loop(0, q_ref.shape[0] // c, chunk, 0)


def _gdn_chunk(q, k, kb, vb, gb, proj3, norm_w, lblk=512, passes=3):
    b, s, w = q.shape
    dh = GDN_HEAD_DIM
    lblk = min(lblk, s)
    seq = pl.BlockSpec((None, lblk, dh), lambda bi, hi, i: (bi, i, hi))
    return pl.pallas_call(
        functools.partial(_gdn_chunk_kernel, passes=passes),
        grid=(b, GDN_HEADS, s // lblk),
        in_specs=[seq] * 5 + [pl.BlockSpec((None, lblk, dh), lambda bi, hi, i: (bi, i, (C_GDN + 3 * w) // dh + hi)),
                              pl.BlockSpec((1, dh), lambda bi, hi, i: (0, 0))],
        out_specs=seq,
        out_shape=jax.ShapeDtypeStruct((b, s, w), BF16),
        scratch_shapes=[pltpu.VMEM((dh, dh), F32)],
        compiler_params=_cp("parallel", "parallel", "arbitrary"),
        name="gdn_chunk",
    )(q, k, kb, vb, gb, proj3, norm_w.reshape(1, dh))


def _merge_kernel(x_ref, oa_ref, ob_ref, oc_ref, ga_ref, gb_ref, gc_ref, wb_ref, wo_ref, o_ref):
    merged = None
    for j, (br, gate) in enumerate(((oa_ref, ga_ref), (ob_ref, gb_ref), (oc_ref, gc_ref))):
        t = _sigmoid(gate[...]) * _dot(br[...], wb_ref[j])
        merged = t if merged is None else merged + t
    o_ref[...] = x_ref[...] + _dot(merged.astype(BF16), wo_ref[...])


def _merge(x2, o_a, o_b, o_c, proj, w_branch, w_out, tm=512):
    t, d = x2.shape
    w = MIX_W
    tm = min(tm, t)
    br = pl.BlockSpec((tm, w), lambda i: (i, 0))
    gate = lambda j: pl.BlockSpec((tm, d), lambda i: (i, C_GATE // d + j))
    return pl.pallas_call(
        _merge_kernel,
        grid=(t // tm,),
        in_specs=[pl.BlockSpec((tm, d), lambda i: (i, 0)), br, br, br, gate(0), gate(1), gate(2),
                  pl.BlockSpec((3, w, d), lambda i: (0, 0, 0)),
                  pl.BlockSpec((d, d), lambda i: (0, 0))],
        out_specs=pl.BlockSpec((tm, d), lambda i: (i, 0)),
        out_shape=jax.ShapeDtypeStruct((t, d), F32),
        compiler_params=_cp("parallel"),
        name="merge",
    )(x2, o_a.reshape(t, w), o_b.reshape(t, w), o_c.reshape(t, w), proj, proj, proj,
      w_branch.astype(BF16), w_out.astype(BF16))


def _ffn_kernel(x_ref, g_ref, w1_ref, w2_ref, o_ref, hn_ref):
    j = pl.program_id(1)

    @pl.when(j == 0)
    def _():
        x = x_ref[...]
        ms = jnp.mean(x * x, axis=-1, keepdims=True)
        hn_ref[...] = (x * lax.rsqrt(ms + EPS) * g_ref[...]).astype(BF16)
        o_ref[...] = x

    h1 = jnp.maximum(_dot(hn_ref[...], w1_ref[...]), 0.0)
    o_ref[...] += _dot((h1 * h1).astype(BF16), w2_ref[...])


def _ffn(x2, g, w1, w2, tm=1024, tf=512):
    t, d = x2.shape
    f = w1.shape[1]
    tm = min(tm, t)
    return pl.pallas_call(
        _ffn_kernel,
        grid=(t // tm, f // tf),
        in_specs=[pl.BlockSpec((tm, d), lambda i, j: (i, 0)),
                  pl.BlockSpec((1, d), lambda i, j: (0, 0)),
                  pl.BlockSpec((d, tf), lambda i, j: (0, j)),
                  pl.BlockSpec((tf, d), lambda i, j: (j, 0))],
        out_specs=pl.BlockSpec((tm, d), lambda i, j: (i, 0)),
        out_shape=jax.ShapeDtypeStruct((t, d), F32),
        scratch_shapes=[pltpu.VMEM((tm, d), BF16)],
        compiler_params=_cp("parallel", "arbitrary"),
        name="ffn",
    )(x2, g.reshape(1, d), w1.astype(BF16), w2.astype(BF16))


def kernel(x, norm_mix_g, w_in, nsa_q_norm, nsa_k_norm, nsa_cmp_pos, nsa_cmp_w1, nsa_cmp_w2, rwkv_mu, rwkv_w0, rwkv_w_up, rwkv_a0, rwkv_a_up, rwkv_g_up, rwkv_k_k, rwkv_k_a, rwkv_r_k, rwkv_ln_w, rwkv_ln_b, rwkv_v0, rwkv_vres_down, rwkv_vres_up, gdn_conv_w, gdn_a_log, gdn_dt_bias, gdn_norm_w, w_branch, w_out, norm_ffn_g, w_ff1, w_ff2):
    b, s, d = x.shape
    depth = w_in.shape[0]
    perm = jnp.asarray(_proj_perm())
    x2 = x.reshape(b * s, d)
    v_first = None
    for i in range(depth):
        vd = rwkv_vres_down[i - 1] if i > 0 else jnp.zeros((d, 32), F32)
        w_ext = jnp.concatenate([w_in[i], vd, jnp.zeros((d, 1), F32)], axis=1)
        w_all = jnp.take(w_ext, perm, axis=1).astype(BF16)
        proj = _rms_matmul(x2, norm_mix_g[i], w_all)
        proj3 = proj.reshape(b, s, N_PROJ)

        o_a = _nsa(proj3, nsa_q_norm[i], nsa_k_norm[i], nsa_cmp_pos[i], nsa_cmp_w1[i], nsa_cmp_w2[i])

        vres = None if i == 0 else (rwkv_v0[i - 1], rwkv_vres_up[i - 1])
        r, k, v, a, lw, g = _rwkv_prep(proj3, rwkv_mu[i], rwkv_w0[i], rwkv_w_up[i], rwkv_a0[i], rwkv_a_up[i],
                                       rwkv_g_up[i], v_first, vres)
        if i == 0:
            v_first = v
        o_b = _rwkv_chunk(r, k, v, a, lw, g, rwkv_k_k[i], rwkv_k_a[i], rwkv_r_k[i], rwkv_ln_w[i], rwkv_ln_b[i])

        gq, gk, gkb, gvb, ggb = _gdn_prep(proj3, gdn_conv_w[i], gdn_a_log[i], gdn_dt_bias[i])
        o_c = _gdn_chunk(gq, gk, gkb, gvb, ggb, proj3, gdn_norm_w[i])

        x2 = _merge(x2, o_a, o_b, o_c, proj, w_branch[i], w_out[i])
        x2 = _ffn(x2, norm_ffn_g[i], w_ff1[i], w_ff2[i])
    return x2.reshape(b, s, d)
```

```python
import functools
import math

import numpy as np
import jax
import jax.numpy as jnp
from jax import lax
from jax.experimental import pallas as pl
from jax.experimental.pallas import tpu as pltpu

F32 = jnp.float32
BF16 = jnp.bfloat16

D_MODEL = 1024
MIX_W = 512
NSA_HEAD_DIM = 64
NSA_KV_GROUPS = 2
NSA_HPG = 4
NSA_CMP_BLOCK = 32
NSA_CMP_STRIDE = 16
NSA_CMP_HIDDEN = 256
NSA_SEL_BLOCK = 64
NSA_N_SEL = 16
NSA_WINDOW = 512
Q_BLOCK = 128
NSA_IN = 1304
RWKV_HEADS = 8
RWKV_HEAD_DIM = 64
RWKV_LN_EPS = 64e-5
RWKV_IN = 1792
GDN_HEADS = 4
GDN_HEAD_DIM = 128
GDN_CONV = 4
GDN_IN = 2056
D_IN = 8224
D_FF = 4096
EPS = 1e-6
NEG = -1e30
FORCE = 1e4

C_KV = 0
C_Q = 768
C_SM0 = 1280
C_SM1 = 1408
C_RKV = 1536
C_GDN = 3072
C_GATE = 5120
C_LR = 8192
N_PROJ = 8448
VMEM_LIMIT = 48 * 1024 * 1024


def _proj_perm():
    zero = D_IN + 32
    perm = np.full((N_PROJ,), zero, np.int32)
    perm[C_KV:C_KV + 768] = np.arange(512, 1280)
    perm[C_Q:C_Q + 512] = np.arange(0, 512)
    perm[C_SM0:C_SM0 + 12] = 1280 + np.arange(12)
    perm[C_SM1:C_SM1 + 12] = 1292 + np.arange(12)
    rw = NSA_IN
    gd = NSA_IN + RWKV_IN
    gt = gd + GDN_IN
    perm[C_SM0 + 16:C_SM0 + 24] = gd + 2048 + np.arange(8)
    perm[C_SM0 + 32:C_SM0 + 64] = D_IN + np.arange(32)
    perm[C_RKV:C_RKV + 1536] = rw + np.arange(1536)
    perm[C_LR:C_LR + 256] = rw + 1536 + np.arange(256)
    perm[C_GDN:C_GDN + 2048] = gd + np.arange(2048)
    perm[C_GATE:C_GATE + 3072] = gt + np.arange(3072)
    return perm


def _cp(*sem):
    return pltpu.CompilerParams(dimension_semantics=sem, vmem_limit_bytes=VMEM_LIMIT)


def _dot(a, b):
    return jnp.dot(a, b, preferred_element_type=F32)


def _dot_nt(a, b):
    return lax.dot_general(a, b, (((1,), (1,)), ((), ())), preferred_element_type=F32)


def _dot_tn(a, b):
    return lax.dot_general(a, b, (((0,), (0,)), ((), ())), preferred_element_type=F32)


def _split(a):
    hi = a.astype(BF16)
    lo = (a - hi.astype(F32)).astype(BF16)
    return hi, lo


def _mm3(a, b):
    ah, al = _split(a)
    bh, bl = _split(b)
    return _dot(ah, bh) + (_dot(ah, bl) + _dot(al, bh))


def _sum01(m, x):
    hi = x.astype(BF16)
    r1 = x - hi.astype(F32)
    mid = r1.astype(BF16)
    lo = (r1 - mid.astype(F32)).astype(BF16)
    return _dot(m, hi) + (_dot(m, mid) + _dot(m, lo))


def _segsum(x, seg):
    hi, lo = _split(x)
    return _dot(hi, seg) + _dot(lo, seg)


CHUNK = 64
HEADS_PER_PASS = 4


def _chunk_consts(ts):
    r = np.arange(ts)
    same = (r[:, None] // CHUNK) == (r[None, :] // CHUNK)
    ltri = same & (r[:, None] >= r[None, :])
    utri = same & (r[:, None] < r[None, :])
    nck = ts // CHUNK
    tot = np.zeros((max(nck, 8), ts), bool)
    tot[:nck] = (r[None, :] // CHUNK) == np.arange(nck)[:, None]
    return [jnp.asarray(m, BF16) for m in (ltri, utri, same, tot)]


def _block_mask(rows, cols, rblk, cblk):
    r = lax.broadcasted_iota(jnp.int32, (rows, cols), 0) // rblk
    c = lax.broadcasted_iota(jnp.int32, (rows, cols), 1) // cblk
    return r == c


def _cat_tri_masks():
    w = HEADS_PER_PASS * CHUNK
    t = lax.broadcasted_iota(jnp.int32, (CHUNK, w), 0)
    s = lax.broadcasted_iota(jnp.int32, (CHUNK, w), 1) % CHUNK
    return t >= s, t > s, t == s


def _bd_tile(y, bd):
    return jnp.concatenate([y] * HEADS_PER_PASS, axis=0) * bd


def _unit_lower_inverse_cat(a, eye, bd):
    p = -a
    t = jnp.where(eye, 1.0, 0.0) + p
    for _ in range(int(math.log2(CHUNK)) - 1):
        pb = p.astype(BF16)
        p = _dot(pb, _bd_tile(pb, bd))
        t = t + _dot(t.astype(BF16), _bd_tile(p.astype(BF16), bd))
    return t


def _softplus(z):
    return jnp.maximum(z, 0.0) + jnp.log1p(jnp.exp(-jnp.abs(z)))


def _sigmoid(z):
    return 1.0 / (1.0 + jnp.exp(-z))


def _rms_matmul_kernel(x_ref, g_ref, w_ref, o_ref, hn_ref):
    @pl.when(pl.program_id(1) == 0)
    def _():
        x = x_ref[...]
        ms = jnp.mean(x * x, axis=-1, keepdims=True)
        hn_ref[...] = (x * lax.rsqrt(ms + EPS) * g_ref[...]).astype(BF16)

    o_ref[...] = _dot(hn_ref[...], w_ref[...])


def _rms_matmul(x2, g, w, tm=512, tn=768):
    t, d = x2.shape
    n = w.shape[1]
    tm = min(tm, t)
    return pl.pallas_call(
        _rms_matmul_kernel,
        grid=(t // tm, n // tn),
        in_specs=[pl.BlockSpec((tm, d), lambda i, j: (i, 0)),
                  pl.BlockSpec((1, d), lambda i, j: (0, 0)),
                  pl.BlockSpec((d, tn), lambda i, j: (0, j))],
        out_specs=pl.BlockSpec((tm, tn), lambda i, j: (i, j)),
        out_shape=jax.ShapeDtypeStruct((t, n), F32),
        scratch_shapes=[pltpu.VMEM((tm, d), BF16)],
        compiler_params=_cp("parallel", "arbitrary"),
        name="rms_proj",
    )(x2, g.reshape(1, d), w)


def _nsa_kvprep_kernel(kv_ref, kg_ref, kc_ref, vc_ref, ks_ref, vs_ref, kw_ref, vw_ref):
    outs = (kc_ref, vc_ref, ks_ref, vs_ref, kw_ref, vw_ref)
    for j in range(6):
        for g in range(NSA_KV_GROUPS):
            lo = j * 128 + g * NSA_HEAD_DIM
            piece = kv_ref[:, lo:lo + NSA_HEAD_DIM]
            if j in (2, 4):
                gain = kg_ref[j // 2:j // 2 + 1, :]
                ms = jnp.mean(piece * piece, axis=-1, keepdims=True)
                piece = piece * lax.rsqrt(ms + EPS) * gain
            outs[j][g] = piece.astype(outs[j].dtype)


def _nsa_kvprep(proj3, k_g, ts=1024):
    b, s, _ = proj3.shape
    g, dh = NSA_KV_GROUPS, NSA_HEAD_DIM
    out_spec = pl.BlockSpec((None, g, ts, dh), lambda bi, i: (bi, 0, i, 0))
    shp = lambda dt: jax.ShapeDtypeStruct((b, g, s, dh), dt)
    return pl.pallas_call(
        _nsa_kvprep_kernel,
        grid=(b, s // ts),
        in_specs=[pl.BlockSpec((None, ts, 768), lambda bi, i: (bi, i, C_KV // 768)),
                  pl.BlockSpec((3, dh), lambda bi, i: (0, 0))],
        out_specs=[out_spec] * 6,
        out_shape=[shp(F32), shp(F32), shp(BF16), shp(BF16), shp(BF16), shp(BF16)],
        compiler_params=_cp("parallel", "parallel"),
        name="nsa_kvprep",
    )(proj3, k_g)


def _nsa_compress_kernel(x_ref, w1_ref, pos_ref, w2_ref, g_ref, o_ref, *, normalize):
    x = x_ref[...].astype(BF16)
    half = x.shape[1]
    a = _dot(x, w1_ref[0:half, :])
    bm = _dot(x, w1_ref[half:2 * half, :])
    bias = _dot(pos_ref[...].astype(BF16), w1_ref[...])[0:1, :]
    n = x.shape[0]
    hid = a + pltpu.roll(bm, n - 1, 0) + bias
    hid = hid * _sigmoid(hid)
    out = _dot(hid.astype(BF16), w2_ref[...])
    if normalize:
        ms = jnp.mean(out * out, axis=-1, keepdims=True)
        out = out * lax.rsqrt(ms + EPS) * g_ref[...]
    o_ref[...] = out.astype(o_ref.dtype)


def _nsa_compress(xh, w1, pos, w2, gain, normalize):
    bg, n, half = xh.shape
    dh = NSA_HEAD_DIM
    pos8 = jnp.broadcast_to(pos.reshape(1, 2 * half), (8, 2 * half))
    return pl.pallas_call(
        functools.partial(_nsa_compress_kernel, normalize=normalize),
        grid=(bg,),
        in_specs=[pl.BlockSpec((None, n, half), lambda i: (i, 0, 0)),
                  pl.BlockSpec((2 * half, NSA_CMP_HIDDEN), lambda i: (0, 0)),
                  pl.BlockSpec((8, 2 * half), lambda i: (0, 0)),
                  pl.BlockSpec((NSA_CMP_HIDDEN, dh), lambda i: (0, 0)),
                  pl.BlockSpec((1, dh), lambda i: (0, 0))],
        out_specs=pl.BlockSpec((None, n, dh), lambda i: (i, 0, 0)),
        out_shape=jax.ShapeDtypeStruct((bg, n, dh), BF16),
        compiler_params=_cp("parallel"),
        name="nsa_compress",
    )(xh, w1.astype(BF16), pos8, w2.astype(BF16), gain.reshape(1, dh))


def _nsa_attn_kernel(q_ref, sm_ref, kc_ref, vc_ref, ks_ref, vs_ref, kw_ref, vw_ref, e_ref, c2s_ref, qg_ref, o_ref,
                     *, n_sel, tk):
    dh, hg, qb = NSA_HEAD_DIM, NSA_HPG, Q_BLOCK
    rows = hg * qb
    blk = pl.program_id(2)
    start = blk * qb
    scale = dh ** -0.5

    xq = q_ref[...]
    qs = []
    for h in range(hg):
        qh = xq[:, h * dh:(h + 1) * dh]
        ms = jnp.mean(qh * qh, axis=-1, keepdims=True)
        qs.append(qh * lax.rsqrt(ms + EPS) * (qg_ref[...] * scale))
    q = jnp.concatenate(qs, axis=0).astype(BF16)
    tq = start + lax.broadcasted_iota(jnp.int32, (qb, 1), 0)
    t_row = jnp.concatenate([tq] * hg, axis=0)

    n_cmp = kc_ref.shape[0]
    s = _dot_nt(q, kc_ref[...])
    cend = lax.broadcasted_iota(jnp.int32, (1, n_cmp), 1) * NSA_CMP_STRIDE + (NSA_CMP_BLOCK - 1)
    valid = cend <= t_row
    s = jnp.where(valid, s, NEG)
    m = jnp.max(s, axis=-1, keepdims=True)
    p = jnp.where(valid, jnp.exp(s - m), 0.0)
    l = jnp.sum(p, axis=-1, keepdims=True)
    p = p * (1.0 / jnp.maximum(l, 1e-30))
    o_c = _dot(p.astype(BF16), vc_ref[...])
    psum = p[0:qb]
    for h in range(1, hg):
        psum = psum + p[h * qb:(h + 1) * qb]
    c2s_t = c2s_ref[...]
    ph = psum.astype(BF16)
    r1 = psum - ph.astype(F32)
    pm = r1.astype(BF16)
    plo = (r1 - pm.astype(F32)).astype(BF16)
    imp_t = _dot_nt(c2s_t, ph) + (_dot_nt(c2s_t, pm) + _dot_nt(c2s_t, plo))

    wlen = NSA_WINDOW + qb
    base = pl.multiple_of(jnp.maximum(start - NSA_WINDOW, 0), qb)
    s = _dot_nt(q, kw_ref[pl.ds(base, wlen), :])
    kp = base + lax.broadcasted_iota(jnp.int32, (1, wlen), 1)
    dist = t_row - kp
    valid = (dist >= 0) & (dist < NSA_WINDOW)
    s = jnp.where(valid, s, NEG)
    m = jnp.max(s, axis=-1, keepdims=True)
    p = jnp.where(valid, jnp.exp(s - m), 0.0)
    l = jnp.sum(p, axis=-1, keepdims=True)
    o_w = _dot(p.astype(BF16), vw_ref[pl.ds(base, wlen), :]) * (1.0 / l)

    n_slc = imp_t.shape[0]
    jr = lax.broadcasted_iota(jnp.int32, (n_slc, 1), 0)
    tq_l = start + lax.broadcasted_iota(jnp.int32, (1, qb), 1)
    cur = jnp.right_shift(tq_l, int(math.log2(NSA_SEL_BLOCK)))
    forced = (jr == 0) | (jr == cur) | (jr == cur - 1)
    causal = jr * NSA_SEL_BLOCK <= tq_l
    val = jnp.where(forced, FORCE, jnp.where(causal, imp_t, -FORCE))
    jrf = jr.astype(F32)
    sel_t = jnp.zeros((n_slc, qb), F32)
    for _ in range(n_sel):
        mx = jnp.max(val, axis=0, keepdims=True)
        idx = jnp.min(jnp.where(val == mx, jrf, float(n_slc)), axis=0, keepdims=True)
        hit = jrf == idx
        sel_t = jnp.where(hit, 1.0, sel_t)
        val = jnp.where(hit, -jnp.inf, val)
    selb = jnp.transpose(sel_t).astype(BF16)

    def sel_body(kt, carry):
        m_i, l_i, acc = carry
        off = pl.multiple_of(kt * tk, tk)
        k = ks_ref[pl.ds(off, tk), :]
        v = vs_ref[pl.ds(off, tk), :]
        s = _dot_nt(q, k)
        member = _dot(selb, e_ref[:, pl.ds(off, tk)])
        kpos = off + lax.broadcasted_iota(jnp.int32, (1, tk), 1)
        bias1 = jnp.where((member > 0.5) & (kpos <= tq), 0.0, NEG)
        s = s + jnp.concatenate([bias1] * hg, axis=0)
        m_new = jnp.maximum(m_i, jnp.max(s, axis=-1, keepdims=True))
        alpha = jnp.exp(m_i - m_new)
        p = jnp.exp(s - m_new)
        l_new = alpha * l_i + jnp.sum(p, axis=-1, keepdims=True)
        acc = alpha * acc + _dot(p.astype(BF16), v)
        return m_new, l_new, acc

    n_kt = (start + qb + tk - 1) // tk
    init = (jnp.full((rows, 1), NEG, F32), jnp.zeros((rows, 1), F32), jnp.zeros((rows, dh), F32))
    _, l_s, acc_s = lax.fori_loop(0, n_kt, sel_body, init)
    o_s = acc_s * (1.0 / l_s)

    gs = _sigmoid(sm_ref[...])
    for h in range(hg):
        r = slice(h * qb, (h + 1) * qb)
        o = (gs[:, 3 * h:3 * h + 1] * o_c[r] + gs[:, 3 * h + 1:3 * h + 2] * o_s[r]
             + gs[:, 3 * h + 2:3 * h + 3] * o_w[r])
        o_ref[:, h * dh:(h + 1) * dh] = o.astype(o_ref.dtype)


def _nsa(proj3, q_g, k_g, cmp_pos, cmp_w1, cmp_w2):
    b, s, _ = proj3.shape
    g, dh, qb = NSA_KV_GROUPS, NSA_HEAD_DIM, Q_BLOCK
    kc_raw, vc_raw, ks, vs, kw, vw = _nsa_kvprep(proj3, k_g, ts=min(1024, s))
    n_half = s // NSA_CMP_STRIDE
    half = NSA_CMP_STRIDE * dh
    kc = _nsa_compress(kc_raw.reshape(b * g, n_half, half), cmp_w1[0], cmp_pos[0], cmp_w2[0], k_g[0], True)
    vc = _nsa_compress(vc_raw.reshape(b * g, n_half, half), cmp_w1[1], cmp_pos[1], cmp_w2[1], k_g[0], False)
    kc = kc.reshape(b, g, n_half, dh)
    vc = vc.reshape(b, g, n_half, dh)

    n_slc = s // NSA_SEL_BLOCK
    n_sel = min(NSA_N_SEL, n_slc)
    tk = 512
    e = (np.arange(s)[None, :] // NSA_SEL_BLOCK == np.arange(n_slc)[:, None]).astype(np.float32)
    cmp_start = np.arange(n_half) * NSA_CMP_STRIDE
    slc_start = np.arange(n_slc) * NSA_SEL_BLOCK
    overlap = np.clip(np.minimum(cmp_start[:, None] + NSA_CMP_BLOCK, slc_start[None, :] + NSA_SEL_BLOCK)
                      - np.maximum(cmp_start[:, None], slc_start[None, :]), 0, None)
    c2s = (overlap / NSA_CMP_BLOCK).T

    kvc_spec = pl.BlockSpec((None, None, n_half, dh), lambda bi, gi, i: (bi, gi, 0, 0))
    kv_spec = pl.BlockSpec((None, None, s, dh), lambda bi, gi, i: (bi, gi, 0, 0))
    return pl.pallas_call(
        functools.partial(_nsa_attn_kernel, n_sel=n_sel, tk=tk),
        grid=(b, g, s // qb),
        in_specs=[pl.BlockSpec((None, qb, 256), lambda bi, gi, i: (bi, i, C_Q // 256 + gi)),
                  pl.BlockSpec((None, qb, 128), lambda bi, gi, i: (bi, i, C_SM0 // 128 + gi)),
                  kvc_spec, kvc_spec, kv_spec, kv_spec, kv_spec, kv_spec,
                  pl.BlockSpec((n_slc, s), lambda bi, gi, i: (0, 0)),
                  pl.BlockSpec((n_slc, n_half), lambda bi, gi, i: (0, 0)),
                  pl.BlockSpec((1, dh), lambda bi, gi, i: (0, 0))],
        out_specs=pl.BlockSpec((None, qb, 256), lambda bi, gi, i: (bi, i, gi)),
        out_shape=jax.ShapeDtypeStruct((b, s, MIX_W), BF16),
        compiler_params=_cp("parallel", "parallel", "arbitrary"),
        name="nsa_attn",
    )(proj3, proj3, kc, vc, ks, vs, kw, vw, jnp.asarray(e, BF16), jnp.asarray(c2s, BF16), q_g.reshape(1, dh))


def _shift_rows(x, prev_row, first):
    rolled = pltpu.roll(x, 1, 0)
    row = lax.broadcasted_iota(jnp.int32, x.shape, 0)
    prev = jnp.where(first, 0.0, prev_row)
    return jnp.where(row == 0, prev, rolled)


PREP_TILE = 512


def _rwkv_prep_kernel(*refs, has_vres):
    (rkv_ref, rkvp_ref, lr_ref, lrp_ref, mu1_ref, mu2_ref, w0_ref, wup_ref, a0_ref, aup_ref, gup_ref,
     kkp_ref, kap_ref, rkp_ref, seg_ref, ltri_ref, utri_ref, tot_ref) = refs[:18]
    if has_vres:
        sm_ref, vf_ref, v0_ref, vu_ref = refs[18:22]
        outs = refs[22:]
    else:
        outs = refs[18:]
    kapo_ref, beto_ref, kto_ref, rto_ref, ktc_ref, betc_ref, vo_ref, bon_ref, g_ref, ec_ref = outs[:10]
    first = pl.program_id(1) == 0
    w = MIX_W
    c = rkv_ref[...]
    c = c + (_shift_rows(c, rkvp_ref[7:8, :], first) - c) * mu1_ref[...]
    lr = lr_ref[...]
    lr = lr + (_shift_rows(lr, lrp_ref[7:8, :], first) - lr) * mu2_ref[...]
    r, k, v = c[:, 0:w], c[:, w:2 * w], c[:, 2 * w:3 * w]
    wd, ad, gd = lr[:, 0:64], lr[:, 64:128], lr[:, 128:256]
    wlog = -_softplus(-(w0_ref[...] + _mm3(jnp.tanh(wd), wup_ref[...]))) - 0.5
    lw = -jnp.exp(wlog)
    a = _sigmoid(a0_ref[...] + _mm3(ad, aup_ref[...]))
    g_ref[...] = _mm3(_sigmoid(gd), gup_ref[...]).astype(g_ref.dtype)
    if has_vres:
        mix = _sigmoid(v0_ref[...] + _mm3(sm_ref[:, 32:64], vu_ref[...]))
        v = v + (vf_ref[...] - v) * mix
    else:
        outs[10][...] = v
    seg = seg_ref[...]
    kk = k * kkp_ref[...]
    kk = kk * lax.rsqrt(_segsum(kk * kk, seg) + EPS)
    k = k * (1.0 + (a - 1.0) * kap_ref[...])
    kka = kk * a
    lcum = _sum01(ltri_ref[...], lw)
    e_suf = jnp.exp(_sum01(utri_ref[...], lw))
    e_inv = jnp.exp(-lcum)
    kapo_ref[...] = (kk * jnp.exp(lcum - lw)).astype(BF16)
    beto_ref[...] = (kka * e_inv).astype(BF16)
    kto_ref[...] = (k * e_inv).astype(BF16)
    rto_ref[...] = (r * jnp.exp(lcum)).astype(BF16)
    ktc_ref[...] = (k * e_suf).astype(BF16)
    betc_ref[...] = (kka * e_suf).astype(BF16)
    vo_ref[...] = v.astype(BF16)
    bon_ref[...] = (_segsum(r * k * rkp_ref[...], seg) * v).astype(BF16)
    ec_ref[...] = jnp.exp(_sum01(tot_ref[...], lw))[0:ec_ref.shape[0]]


def _rwkv_prep(proj3, mu, w0, w_up, a0, a_up, g_up, k_k, k_a, r_k, v_first, vres):
    b, s, _ = proj3.shape
    w = MIX_W
    has_vres = vres is not None
    ts = PREP_TILE
    nrb = ts // 8
    nck = ts // CHUNK
    cur = lambda wd, cb: pl.BlockSpec((None, ts, wd), lambda bi, i: (bi, i, cb))
    prev = lambda wd, cb: pl.BlockSpec((None, 8, wd), lambda bi, i: (bi, jnp.maximum(i * nrb - 1, 0), cb))
    full = lambda shp: pl.BlockSpec(shp, lambda bi, i: (0,) * len(shp))
    ltri, utri, _, tot = _chunk_consts(ts)
    hd = np.arange(w) // RWKV_HEAD_DIM
    seg = jnp.asarray(hd[:, None] == hd[None, :], BF16)
    in_specs = [cur(1536, C_RKV // 1536), prev(1536, C_RKV // 1536), cur(256, C_LR // 256), prev(256, C_LR // 256),
                full((1, 1536)), full((1, 256)), full((1, w)), full((64, w)), full((1, w)), full((64, w)),
                full((128, w)), full((1, w)), full((1, w)), full((1, w)), full((w, w)), full((ts, ts)),
                full((ts, ts)), full(tot.shape)]
    args = [proj3, proj3, proj3, proj3, mu[:1536].reshape(1, 1536), mu[1536:].reshape(1, 256), w0.reshape(1, w),
            w_up, a0.reshape(1, w), a_up, g_up, k_k.reshape(1, w), k_a.reshape(1, w), r_k.reshape(1, w),
            seg, ltri, utri, tot]
    if has_vres:
        v0, vu = vres
        in_specs += [cur(128, C_SM0 // 128), pl.BlockSpec((None, ts, w), lambda bi, i: (bi, i, 0)),
                     full((1, w)), full((32, w))]
        args += [proj3, v_first, v0.reshape(1, w), vu]
    seq = pl.BlockSpec((None, ts, w), lambda bi, i: (bi, i, 0))
    out_specs = [seq] * 9 + [pl.BlockSpec((None, nck, w), lambda bi, i: (bi, i, 0))]
    out_shape = [jax.ShapeDtypeStruct((b, s, w), BF16)] * 9 + [jax.ShapeDtypeStruct((b, s // CHUNK, w), F32)]
    if not has_vres:
        out_specs.append(seq)
        out_shape.append(jax.ShapeDtypeStruct((b, s, w), F32))
    return pl.pallas_call(
        functools.partial(_rwkv_prep_kernel, has_vres=has_vres),
        grid=(b, s // ts),
        in_specs=in_specs,
        out_specs=out_specs,
        out_shape=out_shape,
        compiler_params=_cp("parallel", "parallel"),
        name="rwkv_prep",
    )(*args)


def _rwkv_chunk_kernel(kap_ref, bet_ref, kt_ref, rt_ref, ktc_ref, betc_ref, v_ref, bon_ref, g_ref, ec_ref,
                       lnw_ref, lnb_ref, o_ref, s_ref):
    c = CHUNK
    gw = HEADS_PER_PASS * RWKV_HEAD_DIM
    nseq = kap_ref.shape[0]
    ngrp = kap_ref.shape[2] // gw

    @pl.when(pl.program_id(1) == 0)
    def _():
        s_ref[...] = jnp.zeros_like(s_ref)

    incl, strict, eye = _cat_tri_masks()
    bd_b = _block_mask(gw, gw, RWKV_HEAD_DIM, RWKV_HEAD_DIM)
    bd = jnp.where(bd_b, 1.0, 0.0).astype(BF16)

    def chunk(ci, carry):
        rows = pl.ds(pl.multiple_of(ci * c, c), c)
        for sq in range(nseq):
            ec = ec_ref[sq, pl.ds(ci, 1), :]
            for gi in range(ngrp):
                ls = slice(gi * gw, (gi + 1) * gw)
                kap = kap_ref[sq, rows, ls]
                rt = rt_ref[sq, rows, ls]
                v = v_ref[sq, rows, ls]
                lhs = jnp.concatenate([kap, rt], axis=0)
                ab = _dot_nt(lhs, _bd_tile(bet_ref[sq, rows, ls], bd))
                ak = _dot_nt(lhs, _bd_tile(kt_ref[sq, rows, ls], bd))
                a_kb = jnp.where(strict, ab[0:c], 0.0)
                a_rb = jnp.where(incl, ab[c:2 * c], 0.0)
                a_kk = jnp.where(strict, ak[0:c], 0.0)
                a_rk = jnp.where(incl, ak[c:2 * c], 0.0)
                tinv = _unit_lower_inverse_cat(a_kb, eye, bd)
                st = s_ref[sq, gi]
                kr = _dot_nt(lhs, st.astype(BF16))
                vbd = _bd_tile(v, bd)
                u = _dot(tinv.astype(BF16), _bd_tile((kr[0:c] + _dot(a_kk.astype(BF16), vbd)).astype(BF16), bd))
                ub = u.astype(BF16)
                y = kr[c:2 * c] + _dot(a_rk.astype(BF16), vbd) - _dot(a_rb.astype(BF16), _bd_tile(ub, bd))
                upd = _dot_tn(jnp.concatenate([v, ub], axis=0),
                              jnp.concatenate([ktc_ref[sq, rows, ls], -betc_ref[sq, rows, ls]], axis=0))
                s_ref[sq, gi] = st * ec[:, ls] + jnp.where(bd_b, upd, 0.0)
                mom = _segsum(jnp.concatenate([y, y * y], axis=0), bd) * (1.0 / RWKV_HEAD_DIM)
                mean = mom[0:c]
                var = mom[c:2 * c] - mean * mean
                y = (y - mean) * lax.rsqrt(var + RWKV_LN_EPS) * lnw_ref[:, ls] + lnb_ref[:, ls]
                y = (y + bon_ref[sq, rows, ls].astype(F32)) * g_ref[sq, rows, ls].astype(F32)
                o_ref[sq, rows, ls] = y.astype(o_ref.dtype)
        return carry

    lax.fori_loop(0, kap_ref.shape[1] // c, chunk, 0)


SEQ_PER_STEP = 2
REC_BLOCK = 512


def _rwkv_chunk(ops, ec, ln_w, ln_b):
    b, s, w = ops[0].shape
    nsq = SEQ_PER_STEP if b % SEQ_PER_STEP == 0 else 1
    lblk = REC_BLOCK
    seq = pl.BlockSpec((nsq, lblk, w), lambda bi, i: (bi, i, 0))
    par = pl.BlockSpec((1, w), lambda bi, i: (0, 0))
    gw = HEADS_PER_PASS * RWKV_HEAD_DIM
    return pl.pallas_call(
        _rwkv_chunk_kernel,
        grid=(b // nsq, s // lblk),
        in_specs=[seq] * 9 + [pl.BlockSpec((nsq, lblk // CHUNK, w), lambda bi, i: (bi, i, 0)), par, par],
        out_specs=seq,
        out_shape=jax.ShapeDtypeStruct((b, s, w), BF16),
        scratch_shapes=[pltpu.VMEM((nsq, w // gw, gw, gw), F32)],
        compiler_params=_cp("parallel", "arbitrary"),
        name="rwkv_chunk",
    )(*ops, ec, ln_w.reshape(1, w), ln_b.reshape(1, w))


def _gdn_prep_kernel(x_ref, xp_ref, sm_ref, z_ref, cw_ref, alog_ref, dtb_ref, ltri_ref, utri_ref, same_ref, tot_ref,
                     q_ref, k_ref, kb_ref, vb_ref, kbe_ref, qg_ref, kg_ref, zs_ref, dec_ref, egl_ref):
    first = pl.program_id(1) == 0
    dh = GDN_HEAD_DIM
    w = MIX_W
    x = x_ref[...]
    ts = x.shape[0]
    prev = jnp.where(first, 0.0, xp_ref[...])
    row = lax.broadcasted_iota(jnp.int32, x.shape, 0)
    acc = x * cw_ref[GDN_CONV - 1:GDN_CONV, :]
    for d in range(1, GDN_CONV):
        sh = pltpu.roll(x, d, 0)
        for rr in range(d):
            sh = jnp.where(row == rr, prev[8 - d + rr:8 - d + rr + 1, :], sh)
        acc = acc + sh * cw_ref[GDN_CONV - 1 - d:GDN_CONV - d, :]
    act = acc * _sigmoid(acc)
    sm = sm_ref[...]
    gs = [-jnp.exp(alog_ref[:, h:h + 1]) * _softplus(sm[:, 16 + h:17 + h] + dtb_ref[:, h:h + 1])
          for h in range(GDN_HEADS)]
    gc = jnp.concatenate([jnp.broadcast_to(g, (ts, CHUNK)) for g in gs], axis=1)
    gwide = jnp.concatenate([jnp.broadcast_to(g, (ts, dh)) for g in gs], axis=1)
    gamc = _sum01(ltri_ref[...], gc)
    sufc = _sum01(utri_ref[...], gc)
    t_in = lax.broadcasted_iota(jnp.int32, gc.shape, 0) % CHUNK
    s_in = lax.broadcasted_iota(jnp.int32, gc.shape, 1) % CHUNK
    gamr = _sum01(same_ref[...], jnp.where(t_in == s_in, gamc, 0.0))
    dec_ref[...] = jnp.exp(jnp.where(t_in >= s_in, gamc - gamr, NEG))
    egl_ref[...] = jnp.exp(_sum01(tot_ref[...], gwide))[0:egl_ref.shape[0]]
    for h in range(GDN_HEADS):
        ls = slice(h * dh, (h + 1) * dh)
        q = act[:, h * dh:(h + 1) * dh]
        k = act[:, w + h * dh:w + (h + 1) * dh]
        v = act[:, 2 * w + h * dh:2 * w + (h + 1) * dh]
        q = q * lax.rsqrt(jnp.sum(q * q, axis=-1, keepdims=True) + EPS) * (dh ** -0.5)
        k = k * lax.rsqrt(jnp.sum(k * k, axis=-1, keepdims=True) + EPS)
        beta = _sigmoid(sm[:, 20 + h:21 + h])
        eg = jnp.exp(gamc[:, h * CHUNK:h * CHUNK + 1])
        es = jnp.exp(sufc[:, h * CHUNK:h * CHUNK + 1])
        kb = k * beta
        q_ref[:, ls] = q.astype(BF16)
        k_ref[:, ls] = k.astype(BF16)
        kb_ref[:, ls] = kb.astype(BF16)
        vb_ref[:, ls] = (v * beta).astype(BF16)
        kbe_ref[:, ls] = (kb * eg).astype(BF16)
        qg_ref[:, ls] = (q * eg).astype(BF16)
        kg_ref[:, ls] = (k * es).astype(BF16)
    z = z_ref[...]
    zs_ref[...] = (z * _sigmoid(z)).astype(BF16)


def _gdn_prep(proj3, conv_w, a_log, dt_bias):
    b, s, _ = proj3.shape
    w = MIX_W
    ts = PREP_TILE
    nrb = ts // 8
    nck = ts // CHUNK
    cw = HEADS_PER_PASS * CHUNK
    ltri, utri, same, tot = _chunk_consts(ts)
    full = lambda shp: pl.BlockSpec(shp, lambda bi, i: (0,) * len(shp))
    seq = pl.BlockSpec((None, ts, w), lambda bi, i: (bi, i, 0))
    return pl.pallas_call(
        _gdn_prep_kernel,
        grid=(b, s // ts),
        in_specs=[pl.BlockSpec((None, ts, 3 * w), lambda bi, i: (bi, i, C_GDN // (3 * w))),
                  pl.BlockSpec((None, 8, 3 * w), lambda bi, i: (bi, jnp.maximum(i * nrb - 1, 0), C_GDN // (3 * w))),
                  pl.BlockSpec((None, ts, 128), lambda bi, i: (bi, i, C_SM0 // 128)),
                  pl.BlockSpec((None, ts, w), lambda bi, i: (bi, i, (C_GDN + 3 * w) // w)),
                  full((GDN_CONV, 3 * w)), full((1, GDN_HEADS)), full((1, GDN_HEADS)),
                  full((ts, ts)), full((ts, ts)), full((ts, ts)), full(tot.shape)],
        out_specs=[seq] * 8 + [pl.BlockSpec((None, ts, cw), lambda bi, i: (bi, i, 0)),
                               pl.BlockSpec((None, nck, w), lambda bi, i: (bi, i, 0))],
        out_shape=[jax.ShapeDtypeStruct((b, s, w), BF16)] * 8 + [jax.ShapeDtypeStruct((b, s, cw), F32),
                                                                 jax.ShapeDtypeStruct((b, s // CHUNK, w), F32)],
        compiler_params=_cp("parallel", "parallel"),
        name="gdn_prep",
    )(proj3, proj3, proj3, proj3, conv_w, a_log.reshape(1, GDN_HEADS), dt_bias.reshape(1, GDN_HEADS),
      ltri, utri, same, tot)


def _gdn_chunk_kernel(q_ref, k_ref, kb_ref, vb_ref, kbe_ref, qg_ref, kg_ref, zs_ref, dec_ref, egl_ref, nw_ref,
                      o_ref, s_ref):
    c = CHUNK
    dh = GDN_HEAD_DIM
    w = MIX_W
    pw = 2 * dh
    npair = w // pw
    nseq = q_ref.shape[0]

    @pl.when(pl.program_id(1) == 0)
    def _():
        s_ref[...] = jnp.zeros_like(s_ref)

    _, strict, eye = _cat_tri_masks()
    cw = HEADS_PER_PASS * c
    bd64 = jnp.where(_block_mask(cw, cw, c, c), 1.0, 0.0).astype(BF16)
    bdk = jnp.where(_block_mask(cw, w, c, dh), 1.0, 0.0).astype(BF16)
    bdp_b = _block_mask(pw, pw, dh, dh)

    def chunk(ci, carry):
        rows = pl.ds(pl.multiple_of(ci * c, c), c)
        for sq in range(nseq):
            egl = egl_ref[sq, pl.ds(ci, 1), :]
            lhs = jnp.concatenate([kb_ref[sq, rows, :], q_ref[sq, rows, :]], axis=0)
            aq = _dot_nt(lhs, _bd_tile(k_ref[sq, rows, :], bdk))
            dec = dec_ref[sq, rows, :]
            a_mat = jnp.where(strict, aq[0:c] * dec, 0.0)
            qk = aq[c:2 * c] * dec
            tb = _unit_lower_inverse_cat(a_mat, eye, bd64).astype(BF16)
            u = _dot(tb, _bd_tile(vb_ref[sq, rows, :], bdk))
            wm = _dot(tb, _bd_tile(kbe_ref[sq, rows, :], bdk)).astype(BF16)
            qg = qg_ref[sq, rows, :]
            sts = [s_ref[sq, p] for p in range(npair)]
            stb = [st.astype(BF16) for st in sts]
            ws = jnp.concatenate([_dot(wm[:, p * pw:(p + 1) * pw], stb[p]) for p in range(npair)], axis=1)
            v_new = u - ws
            vnb = v_new.astype(BF16)
            o = (jnp.concatenate([_dot(qg[:, p * pw:(p + 1) * pw], stb[p]) for p in range(npair)], axis=1)
                 + _dot(qk.astype(BF16), _bd_tile(vnb, bdk)))
            kg = kg_ref[sq, rows, :]
            for p in range(npair):
                ps = slice(p * pw, (p + 1) * pw)
                s_ref[sq, p] = sts[p] * egl[:, ps] + jnp.where(bdp_b, _dot_tn(kg[:, ps], vnb[:, ps]), 0.0)
            for h in range(GDN_HEADS):
                ls = slice(h * dh, (h + 1) * dh)
                oh = o[:, ls]
                ms = jnp.mean(oh * oh, axis=-1, keepdims=True)
                o_ref[sq, rows, ls] = (oh * lax.rsqrt(ms + EPS) * nw_ref[...]
                                       * zs_ref[sq, rows, ls].astype(F32)).astype(o_ref.dtype)
        return carry

    lax.fori_loop(0, q_ref.shape[1] // c, chunk, 0)


def _gdn_chunk(ops, dec, egl, norm_w):
    b, s, w = ops[0].shape
    dh = GDN_HEAD_DIM
    nsq = SEQ_PER_STEP if b % SEQ_PER_STEP == 0 else 1
    lblk = REC_BLOCK
    seq = pl.BlockSpec((nsq, lblk, w), lambda bi, i: (bi, i, 0))
    return pl.pallas_call(
        _gdn_chunk_kernel,
        grid=(b // nsq, s // lblk),
        in_specs=[seq] * 8 + [pl.BlockSpec((nsq, lblk, dec.shape[2]), lambda bi, i: (bi, i, 0)),
                              pl.BlockSpec((nsq, lblk // CHUNK, w), lambda bi, i: (bi, i, 0)),
                              pl.BlockSpec((1, dh), lambda bi, i: (0, 0))],
        out_specs=seq,
        out_shape=jax.ShapeDtypeStruct((b, s, w), BF16),
        scratch_shapes=[pltpu.VMEM((nsq, w // (2 * dh), 2 * dh, 2 * dh), F32)],
        compiler_params=_cp("parallel", "arbitrary"),
        name="gdn_chunk",
    )(*ops, dec, egl, norm_w.reshape(1, dh))


def _merge_kernel(x_ref, oa_ref, ob_ref, oc_ref, ga_ref, gb_ref, gc_ref, wb_ref, wo_ref, o_ref):
    merged = None
    for j, (br, gate) in enumerate(((oa_ref, ga_ref), (ob_ref, gb_ref), (oc_ref, gc_ref))):
        t = _sigmoid(gate[...]) * _dot(br[...], wb_ref[j])
        merged = t if merged is None else merged + t
    o_ref[...] = x_ref[...] + _dot(merged.astype(BF16), wo_ref[...])


def _merge(x2, o_a, o_b, o_c, proj, w_branch, w_out, tm=512):
    t, d = x2.shape
    w = MIX_W
    tm = min(tm, t)
    br = pl.BlockSpec((tm, w), lambda i: (i, 0))
    gate = lambda j: pl.BlockSpec((tm, d), lambda i: (i, C_GATE // d + j))
    return pl.pallas_call(
        _merge_kernel,
        grid=(t // tm,),
        in_specs=[pl.BlockSpec((tm, d), lambda i: (i, 0)), br, br, br, gate(0), gate(1), gate(2),
                  pl.BlockSpec((3, w, d), lambda i: (0, 0, 0)),
                  pl.BlockSpec((d, d), lambda i: (0, 0))],
        out_specs=pl.BlockSpec((tm, d), lambda i: (i, 0)),
        out_shape=jax.ShapeDtypeStruct((t, d), F32),
        compiler_params=_cp("parallel"),
        name="merge",
    )(x2, o_a.reshape(t, w), o_b.reshape(t, w), o_c.reshape(t, w), proj, proj, proj,
      w_branch.astype(BF16), w_out.astype(BF16))


def _ffn_kernel(x_ref, g_ref, w1_ref, w2_ref, o_ref, hn_ref):
    j = pl.program_id(1)

    @pl.when(j == 0)
    def _():
        x = x_ref[...]
        ms = jnp.mean(x * x, axis=-1, keepdims=True)
        hn_ref[...] = (x * lax.rsqrt(ms + EPS) * g_ref[...]).astype(BF16)
        o_ref[...] = x

    h1 = jnp.maximum(_dot(hn_ref[...], w1_ref[...]), 0.0)
    o_ref[...] += _dot((h1 * h1).astype(BF16), w2_ref[...])


def _ffn(x2, g, w1, w2, tm=1024, tf=512):
    t, d = x2.shape
    f = w1.shape[1]
    tm = min(tm, t)
    return pl.pallas_call(
        _ffn_kernel,
        grid=(t // tm, f // tf),
        in_specs=[pl.BlockSpec((tm, d), lambda i, j: (i, 0)),
                  pl.BlockSpec((1, d), lambda i, j: (0, 0)),
                  pl.BlockSpec((d, tf), lambda i, j: (0, j)),
                  pl.BlockSpec((tf, d), lambda i, j: (j, 0))],
        out_specs=pl.BlockSpec((tm, d), lambda i, j: (i, 0)),
        out_shape=jax.ShapeDtypeStruct((t, d), F32),
        scratch_shapes=[pltpu.VMEM((tm, d), BF16)],
        compiler_params=_cp("parallel", "arbitrary"),
        name="ffn",
    )(x2, g.reshape(1, d), w1.astype(BF16), w2.astype(BF16))


def kernel(x, norm_mix_g, w_in, nsa_q_norm, nsa_k_norm, nsa_cmp_pos, nsa_cmp_w1, nsa_cmp_w2, rwkv_mu, rwkv_w0, rwkv_w_up, rwkv_a0, rwkv_a_up, rwkv_g_up, rwkv_k_k, rwkv_k_a, rwkv_r_k, rwkv_ln_w, rwkv_ln_b, rwkv_v0, rwkv_vres_down, rwkv_vres_up, gdn_conv_w, gdn_a_log, gdn_dt_bias, gdn_norm_w, w_branch, w_out, norm_ffn_g, w_ff1, w_ff2):
    b, s, d = x.shape
    depth = w_in.shape[0]
    perm = jnp.asarray(_proj_perm())
    x2 = x.reshape(b * s, d)
    v_first = None
    for i in range(depth):
        vd = rwkv_vres_down[i - 1] if i > 0 else jnp.zeros((d, 32), F32)
        w_ext = jnp.concatenate([w_in[i], vd, jnp.zeros((d, 1), F32)], axis=1)
        w_all = jnp.take(w_ext, perm, axis=1).astype(BF16)
        proj = _rms_matmul(x2, norm_mix_g[i], w_all)
        proj3 = proj.reshape(b, s, N_PROJ)

        o_a = _nsa(proj3, nsa_q_norm[i], nsa_k_norm[i], nsa_cmp_pos[i], nsa_cmp_w1[i], nsa_cmp_w2[i])

        vres = None if i == 0 else (rwkv_v0[i - 1], rwkv_vres_up[i - 1])
        rw = _rwkv_prep(proj3, rwkv_mu[i], rwkv_w0[i], rwkv_w_up[i], rwkv_a0[i], rwkv_a_up[i], rwkv_g_up[i],
                        rwkv_k_k[i], rwkv_k_a[i], rwkv_r_k[i], v_first, vres)
        if i == 0:
            v_first = rw[10]
        o_b = _rwkv_chunk(rw[:9], rw[9], rwkv_ln_w[i], rwkv_ln_b[i])

        gd = _gdn_prep(proj3, gdn_conv_w[i], gdn_a_log[i], gdn_dt_bias[i])
        o_c = _gdn_chunk(gd[:8], gd[8], gd[9], gdn_norm_w[i])

        x2 = _merge(x2, o_a, o_b, o_c, proj, w_branch[i], w_out[i])
        x2 = _ffn(x2, norm_ffn_g[i], w_ff1[i], w_ff2[i])
    return x2.reshape(b, s, d)
```

```python
import functools
import math

import numpy as np
import jax
import jax.numpy as jnp
from jax import lax
from jax.experimental import pallas as pl
from jax.experimental.pallas import tpu as pltpu

F32 = jnp.float32
BF16 = jnp.bfloat16

D_MODEL = 1024
MIX_W = 512
NSA_HEAD_DIM = 64
NSA_KV_GROUPS = 2
NSA_HPG = 4
NSA_CMP_BLOCK = 32
NSA_CMP_STRIDE = 16
NSA_CMP_HIDDEN = 256
NSA_SEL_BLOCK = 64
NSA_N_SEL = 16
NSA_WINDOW = 512
Q_BLOCK = 128
NSA_IN = 1304
RWKV_HEADS = 8
RWKV_HEAD_DIM = 64
RWKV_LN_EPS = 64e-5
RWKV_IN = 1792
GDN_HEADS = 4
GDN_HEAD_DIM = 128
GDN_CONV = 4
GDN_IN = 2056
D_IN = 8224
D_FF = 4096
EPS = 1e-6
NEG = -1e30
FORCE = 1e4

C_KV = 0
C_Q = 768
C_SM0 = 1280
C_SM1 = 1408
C_RKV = 1536
C_GDN = 3072
C_GATE = 5120
C_LR = 8192
N_PROJ = 8448
VMEM_LIMIT = 48 * 1024 * 1024


def _proj_perm():
    zero = D_IN + 32
    perm = np.full((N_PROJ,), zero, np.int32)
    perm[C_KV:C_KV + 768] = np.arange(512, 1280)
    perm[C_Q:C_Q + 512] = np.arange(0, 512)
    perm[C_SM0:C_SM0 + 12] = 1280 + np.arange(12)
    perm[C_SM1:C_SM1 + 12] = 1292 + np.arange(12)
    rw = NSA_IN
    gd = NSA_IN + RWKV_IN
    gt = gd + GDN_IN
    perm[C_SM0 + 16:C_SM0 + 24] = gd + 2048 + np.arange(8)
    perm[C_SM0 + 32:C_SM0 + 64] = D_IN + np.arange(32)
    perm[C_RKV:C_RKV + 1536] = rw + np.arange(1536)
    perm[C_LR:C_LR + 256] = rw + 1536 + np.arange(256)
    perm[C_GDN:C_GDN + 2048] = gd + np.arange(2048)
    perm[C_GATE:C_GATE + 3072] = gt + np.arange(3072)
    return perm


def _cp(*sem):
    return pltpu.CompilerParams(dimension_semantics=sem, vmem_limit_bytes=VMEM_LIMIT)


def _dot(a, b):
    return jnp.dot(a, b, preferred_element_type=F32)


def _dot_nt(a, b):
    return lax.dot_general(a, b, (((1,), (1,)), ((), ())), preferred_element_type=F32)


def _dot_tn(a, b):
    return lax.dot_general(a, b, (((0,), (0,)), ((), ())), preferred_element_type=F32)


def _split(a):
    hi = a.astype(BF16)
    lo = (a - hi.astype(F32)).astype(BF16)
    return hi, lo


def _mm3(a, b):
    ah, al = _split(a)
    bh, bl = _split(b)
    return _dot(ah, bh) + (_dot(ah, bl) + _dot(al, bh))


def _sum01(m, x):
    hi = x.astype(BF16)
    r1 = x - hi.astype(F32)
    mid = r1.astype(BF16)
    lo = (r1 - mid.astype(F32)).astype(BF16)
    return _dot(m, hi) + (_dot(m, mid) + _dot(m, lo))


def _segsum(x, seg):
    hi, lo = _split(x)
    return _dot(hi, seg) + _dot(lo, seg)


CHUNK = 64
HEADS_PER_PASS = 4


def _chunk_consts(ts):
    r = np.arange(ts)
    same = (r[:, None] // CHUNK) == (r[None, :] // CHUNK)
    ltri = same & (r[:, None] >= r[None, :])
    utri = same & (r[:, None] < r[None, :])
    nck = ts // CHUNK
    tot = np.zeros((max(nck, 8), ts), bool)
    tot[:nck] = (r[None, :] // CHUNK) == np.arange(nck)[:, None]
    return [jnp.asarray(m, BF16) for m in (ltri, utri, same, tot)]


def _block_mask(rows, cols, rblk, cblk):
    r = lax.broadcasted_iota(jnp.int32, (rows, cols), 0) // rblk
    c = lax.broadcasted_iota(jnp.int32, (rows, cols), 1) // cblk
    return r == c


def _cat_tri_masks():
    w = HEADS_PER_PASS * CHUNK
    t = lax.broadcasted_iota(jnp.int32, (CHUNK, w), 0)
    s = lax.broadcasted_iota(jnp.int32, (CHUNK, w), 1) % CHUNK
    return t >= s, t > s, t == s


def _bd_tile(y, bd):
    return jnp.concatenate([y] * HEADS_PER_PASS, axis=0) * bd


def _softplus(z):
    return jnp.maximum(z, 0.0) + jnp.log1p(jnp.exp(-jnp.abs(z)))


def _sigmoid(z):
    return 1.0 / (1.0 + jnp.exp(-z))


def _rms_matmul_kernel(x_ref, g_ref, w_ref, o_ref, hn_ref):
    @pl.when(pl.program_id(1) == 0)
    def _():
        x = x_ref[...]
        ms = jnp.mean(x * x, axis=-1, keepdims=True)
        hn_ref[...] = (x * lax.rsqrt(ms + EPS) * g_ref[...]).astype(BF16)

    o_ref[...] = _dot(hn_ref[...], w_ref[...])


def _rms_matmul(x2, g, w, tm=1024, tn=768):
    t, d = x2.shape
    n = w.shape[1]
    tm = min(tm, t)
    return pl.pallas_call(
        _rms_matmul_kernel,
        grid=(t // tm, n // tn),
        in_specs=[pl.BlockSpec((tm, d), lambda i, j: (i, 0)),
                  pl.BlockSpec((1, d), lambda i, j: (0, 0)),
                  pl.BlockSpec((d, tn), lambda i, j: (0, j))],
        out_specs=pl.BlockSpec((tm, tn), lambda i, j: (i, j)),
        out_shape=jax.ShapeDtypeStruct((t, n), F32),
        scratch_shapes=[pltpu.VMEM((tm, d), BF16)],
        compiler_params=_cp("parallel", "arbitrary"),
        name="rms_proj",
    )(x2, g.reshape(1, d), w)


def _nsa_kvprep_kernel(kv_ref, kg_ref, kc_ref, vc_ref, ks_ref, vs_ref, kw_ref, vw_ref):
    outs = (kc_ref, vc_ref, ks_ref, vs_ref, kw_ref, vw_ref)
    for j in range(6):
        for g in range(NSA_KV_GROUPS):
            lo = j * 128 + g * NSA_HEAD_DIM
            piece = kv_ref[:, lo:lo + NSA_HEAD_DIM]
            if j in (2, 4):
                gain = kg_ref[j // 2:j // 2 + 1, :]
                ms = jnp.mean(piece * piece, axis=-1, keepdims=True)
                piece = piece * lax.rsqrt(ms + EPS) * gain
            outs[j][g] = piece.astype(outs[j].dtype)


def _nsa_kvprep(proj3, k_g, ts=1024):
    b, s, _ = proj3.shape
    g, dh = NSA_KV_GROUPS, NSA_HEAD_DIM
    out_spec = pl.BlockSpec((None, g, ts, dh), lambda bi, i: (bi, 0, i, 0))
    shp = lambda dt: jax.ShapeDtypeStruct((b, g, s, dh), dt)
    return pl.pallas_call(
        _nsa_kvprep_kernel,
        grid=(b, s // ts),
        in_specs=[pl.BlockSpec((None, ts, 768), lambda bi, i: (bi, i, C_KV // 768)),
                  pl.BlockSpec((3, dh), lambda bi, i: (0, 0))],
        out_specs=[out_spec] * 6,
        out_shape=[shp(F32), shp(F32), shp(BF16), shp(BF16), shp(BF16), shp(BF16)],
        compiler_params=_cp("parallel", "parallel"),
        name="nsa_kvprep",
    )(proj3, k_g)


def _nsa_compress_kernel(x_ref, w1_ref, pos_ref, w2_ref, g_ref, o_ref, *, normalize):
    x = x_ref[...].astype(BF16)
    half = x.shape[1]
    a = _dot(x, w1_ref[0:half, :])
    bm = _dot(x, w1_ref[half:2 * half, :])
    bias = _dot(pos_ref[...].astype(BF16), w1_ref[...])[0:1, :]
    n = x.shape[0]
    hid = a + pltpu.roll(bm, n - 1, 0) + bias
    hid = hid * _sigmoid(hid)
    out = _dot(hid.astype(BF16), w2_ref[...])
    if normalize:
        ms = jnp.mean(out * out, axis=-1, keepdims=True)
        out = out * lax.rsqrt(ms + EPS) * g_ref[...]
    o_ref[...] = out.astype(o_ref.dtype)


def _nsa_compress(xh, w1, pos, w2, gain, normalize):
    bg, n, half = xh.shape
    dh = NSA_HEAD_DIM
    pos8 = jnp.broadcast_to(pos.reshape(1, 2 * half), (8, 2 * half))
    return pl.pallas_call(
        functools.partial(_nsa_compress_kernel, normalize=normalize),
        grid=(bg,),
        in_specs=[pl.BlockSpec((None, n, half), lambda i: (i, 0, 0)),
                  pl.BlockSpec((2 * half, NSA_CMP_HIDDEN), lambda i: (0, 0)),
                  pl.BlockSpec((8, 2 * half), lambda i: (0, 0)),
                  pl.BlockSpec((NSA_CMP_HIDDEN, dh), lambda i: (0, 0)),
                  pl.BlockSpec((1, dh), lambda i: (0, 0))],
        out_specs=pl.BlockSpec((None, n, dh), lambda i: (i, 0, 0)),
        out_shape=jax.ShapeDtypeStruct((bg, n, dh), BF16),
        compiler_params=_cp("parallel"),
        name="nsa_compress",
    )(xh, w1.astype(BF16), pos8, w2.astype(BF16), gain.reshape(1, dh))


SEL_TILE = 256
LOG2E = 1.4426950408889634


def _softmax_cols(s, valid):
    s = jnp.where(valid, s, NEG)
    p = jnp.exp2(s - jnp.max(s, axis=0, keepdims=True))
    return p, jnp.sum(p, axis=0, keepdims=True)


def _nsa_attn_t_kernel(q_ref, sm_ref, kc_ref, vct_ref, ks_ref, vst_ref, kw_ref, vwt_ref, c2s_ref, qg_ref, o_ref,
                       sb_ref, sa_ref, sb2_ref, pa_ref, pb_ref, acc_ref, al_ref, m_ref, l_ref, *, n_sel):
    dh, hg, qb = NSA_HEAD_DIM, NSA_HPG, Q_BLOCK
    cols = hg * qb
    tk = SEL_TILE
    blk = pl.program_id(2)
    start = blk * qb

    xt = jnp.transpose(q_ref[...])
    qs = []
    for h in range(hg):
        xh = xt[h * dh:(h + 1) * dh, :]
        ms = jnp.mean(xh * xh, axis=0, keepdims=True)
        qs.append(xh * lax.rsqrt(ms + EPS) * qg_ref[...])
    qt = jnp.concatenate(qs, axis=1).astype(BF16)
    tq_l =start + lax.broadcasted_iota(jnp.int32, (1, qb), 1)
    heads = [slice(h * qb, (h + 1) * qb) for h in range(hg)]

    n_cmp = kc_ref.shape[0]
    cvalid = (lax.broadcasted_iota(jnp.int32, (n_cmp, 1), 0) * NSA_CMP_STRIDE + (NSA_CMP_BLOCK - 1)) <= tq_l
    o_c, psum = [], None
    for cs in heads:
        p, l = _softmax_cols(_dot(kc_ref[...], qt[:, cs]), cvalid)
        p = p * jnp.where(tq_l >= NSA_CMP_BLOCK - 1, 1.0 / l, 0.0)
        o_c.append(_dot(vct_ref[...], p.astype(BF16)))
        psum = p if psum is None else psum + p
    imp_t = _sum01(c2s_ref[...], psum)

    n_slc = imp_t.shape[0]
    jr = lax.broadcasted_iota(jnp.int32, (n_slc, 1), 0)
    cur = jnp.right_shift(tq_l, int(math.log2(NSA_SEL_BLOCK)))
    forced = (jr == 0) | (jr == cur) | (jr == cur - 1)
    causal = jr * NSA_SEL_BLOCK <= tq_l
    val = jnp.where(forced, -jnp.inf, jnp.where(causal, imp_t, -FORCE))
    jrf = jr.astype(F32)
    bias = jnp.where(forced, 0.0, NEG)

    def select_round(val, bias):
        mx = jnp.max(val, axis=0, keepdims=True)
        idx = jnp.min(jnp.where(val == mx, jrf, float(n_slc)), axis=0, keepdims=True)
        hit = jrf == idx
        return jnp.where(hit, -jnp.inf, val), jnp.where(hit, 0.0, bias)

    wlen = NSA_WINDOW + qb
    base = pl.multiple_of(jnp.maximum(start - NSA_WINDOW, 0), qb)
    dist = tq_l - (base + lax.broadcasted_iota(jnp.int32, (wlen, 1), 0))
    wvalid = (dist >= 0) & (dist < NSA_WINDOW)
    rounds = n_sel - 3
    o_w = []
    for h, cs in enumerate(heads):
        for _ in range(rounds * h // hg, rounds * (h + 1) // hg):
            val, bias = select_round(val, bias)
        p, l = _softmax_cols(_dot(kw_ref[pl.ds(base, wlen), :], qt[:, cs]), wvalid)
        o_w.append(_dot(vwt_ref[:, pl.ds(base, wlen)], p.astype(BF16)) * (1.0 / l))
    sb_ref[...] = bias

    bpt = tk // NSA_SEL_BLOCK
    last = start // tk
    n_pairs = (last + 1) // 2

    def key_off(i):
        return pl.multiple_of(jnp.minimum(i, last) * tk, tk)

    def scores(i):
        return _dot(ks_ref[pl.ds(key_off(i), tk), :], qt)

    def step(tile, prev_tile, next_tile, s_cur, s_nxt, p_cur, p_prev, first):
        s_nxt[...] = scores(next_tile)
        if not first:
            pv = _dot(vst_ref[:, pl.ds(key_off(prev_tile), tk)], p_prev[...])
        rows = [sb_ref[pl.ds(jnp.minimum(tile, last) * bpt + j, 1), :] for j in range(bpt)]
        if first:
            visible = (tile * tk + lax.broadcasted_iota(jnp.int32, (tk, 1), 0)) <= tq_l
        else:
            rows = [jnp.where(tile < last, r, NEG) for r in rows]
        for cs in heads:
            sh = jnp.concatenate([s_cur[j * NSA_SEL_BLOCK:(j + 1) * NSA_SEL_BLOCK, cs] + rows[j]
                                  for j in range(bpt)], axis=0)
            if first:
                sh = jnp.where(visible, sh, NEG)
            m_old = m_ref[:, cs]
            m_new = jnp.maximum(m_old, jnp.max(sh, axis=0, keepdims=True))
            alpha = jnp.exp2(m_old - m_new)
            p = jnp.exp2(sh - m_new)
            p_cur[:, cs] = p.astype(BF16)
            m_ref[:, cs] = m_new
            l_ref[:, cs] = alpha * l_ref[:, cs] + jnp.sum(p, axis=0, keepdims=True)
            if not first:
                acc_ref[:, cs] = al_ref[:, cs] * acc_ref[:, cs] + pv[:, cs]
            al_ref[:, cs] = alpha

    m_ref[...] = jnp.full((1, cols), NEG, F32)
    l_ref[...] = jnp.zeros((1, cols), F32)
    acc_ref[...] = jnp.zeros((dh, cols), F32)
    sa_ref[...] = scores(last)
    step(last, last, 0, sa_ref, sb2_ref, pa_ref, pb_ref, True)

    def sel_body(j, carry):
        t0 = 2 * j
        step(t0, jnp.where(j == 0, last, t0 - 1), t0 + 1, sb2_ref, sa_ref, pb_ref, pa_ref, False)
        step(t0 + 1, t0, t0 + 2, sa_ref, sb2_ref, pa_ref, pb_ref, False)
        return carry

    lax.fori_loop(0, n_pairs, sel_body, 0)
    prev = jnp.where(n_pairs == 0, last, 2 * n_pairs - 1)
    acc = al_ref[...] * acc_ref[...] + _dot(vst_ref[:, pl.ds(key_off(prev), tk)], pa_ref[...])
    o_s = acc * (1.0 / l_ref[...])

    gt = jnp.transpose(_sigmoid(sm_ref[...]))
    outs = []
    for h, cs in enumerate(heads):
        outs.append(gt[3 * h:3 * h + 1, :] * o_c[h] + gt[3 * h + 1:3 * h + 2, :] * o_s[:, cs]
                    + gt[3 * h + 2:3 * h + 3, :] * o_w[h])
    o_ref[...] = jnp.transpose(jnp.concatenate(outs, axis=0)).astype(o_ref.dtype)


def _nsa(proj3, q_g, k_g, cmp_pos, cmp_w1, cmp_w2):
    b, s, _ = proj3.shape
    g, dh, qb = NSA_KV_GROUPS, NSA_HEAD_DIM, Q_BLOCK
    kc_raw, vc_raw, ks, vs, kw, vw = _nsa_kvprep(proj3, k_g, ts=min(1024, s))
    n_half = s // NSA_CMP_STRIDE
    half = NSA_CMP_STRIDE * dh
    kc = _nsa_compress(kc_raw.reshape(b * g, n_half, half), cmp_w1[0], cmp_pos[0], cmp_w2[0], k_g[0], True)
    vc = _nsa_compress(vc_raw.reshape(b * g, n_half, half), cmp_w1[1], cmp_pos[1], cmp_w2[1], k_g[0], False)
    kc = kc.reshape(b, g, n_half, dh)
    vc = vc.reshape(b, g, n_half, dh)

    n_slc = s // NSA_SEL_BLOCK
    n_sel = min(NSA_N_SEL, n_slc)
    cmp_start = np.arange(n_half) * NSA_CMP_STRIDE
    slc_start = np.arange(n_slc) * NSA_SEL_BLOCK
    overlap = np.clip(np.minimum(cmp_start[:, None] + NSA_CMP_BLOCK, slc_start[None, :] + NSA_SEL_BLOCK)
                      - np.maximum(cmp_start[:, None], slc_start[None, :]), 0, None)
    c2s = (overlap / NSA_CMP_BLOCK).T

    k_spec = lambda n: pl.BlockSpec((None, None, n, dh), lambda bi, gi, i: (bi, gi, 0, 0))
    vt_spec = lambda n: pl.BlockSpec((None, None, dh, n), lambda bi, gi, i: (bi, gi, 0, 0))
    tr = lambda v: jnp.swapaxes(v, 2, 3)
    cols = NSA_HPG * qb
    q_gain = jnp.broadcast_to((q_g * (dh ** -0.5 * LOG2E)).reshape(dh, 1), (dh, qb))
    return pl.pallas_call(
        functools.partial(_nsa_attn_t_kernel, n_sel=n_sel),
        grid=(b, g, s // qb),
        in_specs=[pl.BlockSpec((None, qb, 256), lambda bi, gi, i: (bi, i, C_Q // 256 + gi)),
                  pl.BlockSpec((None, qb, 128), lambda bi, gi, i: (bi, i, C_SM0 // 128 + gi)),
                  k_spec(n_half), vt_spec(n_half), k_spec(s), vt_spec(s), k_spec(s), vt_spec(s),
                  pl.BlockSpec((n_slc, n_half), lambda bi, gi, i: (0, 0)),
                  pl.BlockSpec((dh, qb), lambda bi, gi, i: (0, 0))],
        out_specs=pl.BlockSpec((None, qb, 256), lambda bi, gi, i: (bi, i, gi)),
        out_shape=jax.ShapeDtypeStruct((b, s, MIX_W), BF16),
        scratch_shapes=[pltpu.VMEM((n_slc, qb), F32), pltpu.VMEM((SEL_TILE, cols), F32),
                        pltpu.VMEM((SEL_TILE, cols), F32), pltpu.VMEM((SEL_TILE, cols), BF16),
                        pltpu.VMEM((SEL_TILE, cols), BF16), pltpu.VMEM((dh, cols), F32),
                        pltpu.VMEM((1, cols), F32), pltpu.VMEM((1, cols), F32), pltpu.VMEM((1, cols), F32)],
        compiler_params=_cp("parallel", "parallel", "arbitrary"),
        name="nsa_attn",
    )(proj3, proj3, kc, tr(vc), ks, tr(vs), kw, tr(vw), jnp.asarray(c2s, BF16), q_gain)


def _shift_rows(x, prev_row, first):
    rolled = pltpu.roll(x, 1, 0)
    row = lax.broadcasted_iota(jnp.int32, x.shape, 0)
    prev = jnp.where(first, 0.0, prev_row)
    return jnp.where(row == 0, prev, rolled)


PREP_TILE = 512


def _rwkv_prep_kernel(*refs, has_vres):
    (rkv_ref, rkvp_ref, lr_ref, lrp_ref, mu1_ref, mu2_ref, w0_ref, wup_ref, a0_ref, aup_ref, gup_ref,
     kkp_ref, kap_ref, rkp_ref, seg_ref, ltri_ref, utri_ref, tot_ref) = refs[:18]
    if has_vres:
        sm_ref, vf_ref, v0_ref, vu_ref = refs[18:22]
        outs = refs[22:]
    else:
        outs = refs[18:]
    kapo_ref, beto_ref, kto_ref, rto_ref, ktc_ref, betc_ref, vo_ref, bon_ref, g_ref, ec_ref = outs[:10]
    first = pl.program_id(1) == 0
    w = MIX_W
    c = rkv_ref[...]
    c = c + (_shift_rows(c, rkvp_ref[7:8, :], first) - c) * mu1_ref[...]
    lr = lr_ref[...]
    lr = lr + (_shift_rows(lr, lrp_ref[7:8, :], first) - lr) * mu2_ref[...]
    r, k, v = c[:, 0:w], c[:, w:2 * w], c[:, 2 * w:3 * w]
    wd, ad, gd = lr[:, 0:64], lr[:, 64:128], lr[:, 128:256]
    wlog = -_softplus(-(w0_ref[...] + _mm3(jnp.tanh(wd), wup_ref[...]))) - 0.5
    lw = -jnp.exp(wlog)
    a = _sigmoid(a0_ref[...] + _mm3(ad, aup_ref[...]))
    g_ref[...] = _mm3(_sigmoid(gd), gup_ref[...]).astype(g_ref.dtype)
    if has_vres:
        mix = _sigmoid(v0_ref[...] + _mm3(sm_ref[:, 32:64], vu_ref[...]))
        v = v + (vf_ref[...] - v) * mix
    else:
        outs[10][...] = v
    seg = seg_ref[...]
    kk = k * kkp_ref[...]
    kk = kk * lax.rsqrt(_segsum(kk * kk, seg) + EPS)
    k = k * (1.0 + (a - 1.0) * kap_ref[...])
    kka = kk * a
    lcum = _sum01(ltri_ref[...], lw)
    e_suf = jnp.exp(_sum01(utri_ref[...], lw))
    e_inv = jnp.exp(-lcum)
    kapo_ref[...] = (kk * jnp.exp(lcum - lw)).astype(BF16)
    beto_ref[...] = (kka * e_inv).astype(BF16)
    kto_ref[...] = (k * e_inv).astype(BF16)
    rto_ref[...] = (r * jnp.exp(lcum)).astype(BF16)
    ktc_ref[...] = (k * e_suf).astype(BF16)
    betc_ref[...] = (kka * e_suf).astype(BF16)
    vo_ref[...] = v.astype(BF16)
    bon_ref[...] = (_segsum(r * k * rkp_ref[...], seg) * v).astype(BF16)
    ec_ref[...] = jnp.exp(_sum01(tot_ref[...], lw))[0:ec_ref.shape[0]]


def _rwkv_prep(proj3, mu, w0, w_up, a0, a_up, g_up, k_k, k_a, r_k, v_first, vres):
    b, s, _ = proj3.shape
    w = MIX_W
    has_vres = vres is not None
    ts = PREP_TILE
    nrb = ts // 8
    nck = ts // CHUNK
    cur = lambda wd, cb: pl.BlockSpec((None, ts, wd), lambda bi, i: (bi, i, cb))
    prev = lambda wd, cb: pl.BlockSpec((None, 8, wd), lambda bi, i: (bi, jnp.maximum(i * nrb - 1, 0), cb))
    full = lambda shp: pl.BlockSpec(shp, lambda bi, i: (0,) * len(shp))
    ltri, utri, _, tot = _chunk_consts(ts)
    hd = np.arange(w) // RWKV_HEAD_DIM
    seg = jnp.asarray(hd[:, None] == hd[None, :], BF16)
    in_specs = [cur(1536, C_RKV // 1536), prev(1536, C_RKV // 1536), cur(256, C_LR // 256), prev(256, C_LR // 256),
                full((1, 1536)), full((1, 256)), full((1, w)), full((64, w)), full((1, w)), full((64, w)),
                full((128, w)), full((1, w)), full((1, w)), full((1, w)), full((w, w)), full((ts, ts)),
                full((ts, ts)), full(tot.shape)]
    args = [proj3, proj3, proj3, proj3, mu[:1536].reshape(1, 1536), mu[1536:].reshape(1, 256), w0.reshape(1, w),
            w_up, a0.reshape(1, w), a_up, g_up, k_k.reshape(1, w), k_a.reshape(1, w), r_k.reshape(1, w),
            seg, ltri, utri, tot]
    if has_vres:
        v0, vu = vres
        in_specs += [cur(128, C_SM0 // 128), pl.BlockSpec((None, ts, w), lambda bi, i: (bi, i, 0)),
                     full((1, w)), full((32, w))]
        args += [proj3, v_first, v0.reshape(1, w), vu]
    seq = pl.BlockSpec((None, ts, w), lambda bi, i: (bi, i, 0))
    out_specs = [seq] * 9 + [pl.BlockSpec((None, nck, w), lambda bi, i: (bi, i, 0))]
    out_shape = [jax.ShapeDtypeStruct((b, s, w), BF16)] * 9 + [jax.ShapeDtypeStruct((b, s // CHUNK, w), F32)]
    if not has_vres:
        out_specs.append(seq)
        out_shape.append(jax.ShapeDtypeStruct((b, s, w), F32))
    return pl.pallas_call(
        functools.partial(_rwkv_prep_kernel, has_vres=has_vres),
        grid=(b, s // ts),
        in_specs=in_specs,
        out_specs=out_specs,
        out_shape=out_shape,
        compiler_params=_cp("parallel", "parallel"),
        name="rwkv_prep",
    )(*args)


def _rwkv_chunk_kernel(kap_ref, bet_ref, kt_ref, rt_ref, ktc_ref, betc_ref, v_ref, bon_ref, g_ref, ec_ref,
                       lnw_ref, lnb_ref, o_ref, s_ref):
    c = CHUNK
    gw = HEADS_PER_PASS * RWKV_HEAD_DIM
    nseq = kap_ref.shape[0]
    ngrp = kap_ref.shape[2] // gw

    @pl.when(pl.program_id(1) == 0)
    def _():
        s_ref[...] = jnp.zeros_like(s_ref)

    incl, strict, eye = _cat_tri_masks()
    bd_b = _block_mask(gw, gw, RWKV_HEAD_DIM, RWKV_HEAD_DIM)
    bd = jnp.where(bd_b, 1.0, 0.0).astype(BF16)

    chains = [(sq, gi) for sq in range(nseq) for gi in range(ngrp)]
    lanes = [slice(gi * gw, (gi + 1) * gw) for _, gi in chains]
    every = range(len(chains))

    def chunk(ci, carry):
        rows = pl.ds(pl.multiple_of(ci * c, c), c)
        at = lambda ref, n: ref[chains[n][0], rows, lanes[n]]
        v = [at(v_ref, n) for n in every]
        lhs = [jnp.concatenate([at(kap_ref, n), at(rt_ref, n)], axis=0) for n in every]
        ab = [_dot_nt(lhs[n], _bd_tile(at(bet_ref, n), bd)) for n in every]
        ak = [_dot_nt(lhs[n], _bd_tile(at(kt_ref, n), bd)) for n in every]
        p = [-jnp.where(strict, ab[n][0:c], 0.0) for n in every]
        t = [jnp.where(eye, 1.0, 0.0) + p[n] for n in every]
        for _ in range(int(math.log2(c)) - 1):
            pb = [x.astype(BF16) for x in p]
            p = [_dot(pb[n], _bd_tile(pb[n], bd)) for n in every]
            t = [t[n] + _dot(t[n].astype(BF16), _bd_tile(p[n].astype(BF16), bd)) for n in every]
        st = [s_ref[sq, gi] for sq, gi in chains]
        kr = [_dot_nt(lhs[n], st[n].astype(BF16)) for n in every]
        vbd = [_bd_tile(v[n], bd) for n in every]
        x = [kr[n][0:c] + _dot(jnp.where(strict, ak[n][0:c], 0.0).astype(BF16), vbd[n]) for n in every]
        ub = [_dot(t[n].astype(BF16), _bd_tile(x[n].astype(BF16), bd)).astype(BF16) for n in every]
        y = [kr[n][c:2 * c] + _dot(jnp.where(incl, ak[n][c:2 * c], 0.0).astype(BF16), vbd[n])
             - _dot(jnp.where(incl, ab[n][c:2 * c], 0.0).astype(BF16), _bd_tile(ub[n], bd)) for n in every]
        upd = [_dot_tn(jnp.concatenate([v[n], ub[n]], axis=0),
                       jnp.concatenate([at(ktc_ref, n), -at(betc_ref, n)], axis=0)) for n in every]
        mom = [_segsum(jnp.concatenate([y[n], y[n] * y[n]], axis=0), bd) * (1.0 / RWKV_HEAD_DIM) for n in every]
        for n, (sq, gi) in enumerate(chains):
            ec = ec_ref[sq, pl.ds(ci, 1), lanes[n]]
            s_ref[sq, gi] = st[n] * ec + jnp.where(bd_b, upd[n], 0.0)
            mean = mom[n][0:c]
            var = mom[n][c:2 * c] - mean * mean
            yn = (y[n] - mean) * lax.rsqrt(var + RWKV_LN_EPS) * lnw_ref[:, lanes[n]] + lnb_ref[:, lanes[n]]
            yn = (yn + at(bon_ref, n).astype(F32)) * at(g_ref, n).astype(F32)
            o_ref[sq, rows, lanes[n]] = yn.astype(o_ref.dtype)
        return carry

    lax.fori_loop(0, kap_ref.shape[1] // c, chunk, 0)


SEQ_PER_STEP = 4
REC_BLOCK = 256


def _seq_per_step(b):
    return max(n for n in range(1, SEQ_PER_STEP + 1) if b % n == 0)


def _per_chunk_spec(x, nsq):
    b, nc, w = x.shape
    cps = REC_BLOCK // CHUNK
    return x.reshape(b, nc // cps, cps, w), pl.BlockSpec((nsq, None, cps, w), lambda bi, i: (bi, i, 0, 0))


def _rwkv_chunk(ops, ec, ln_w, ln_b):
    b, s, w = ops[0].shape
    nsq = _seq_per_step(b)
    lblk = REC_BLOCK
    seq = pl.BlockSpec((nsq, lblk, w), lambda bi, i: (bi, i, 0))
    par = pl.BlockSpec((1, w), lambda bi, i: (0, 0))
    gw = HEADS_PER_PASS * RWKV_HEAD_DIM
    ec4, ec_spec = _per_chunk_spec(ec, nsq)
    return pl.pallas_call(
        _rwkv_chunk_kernel,
        grid=(b // nsq, s // lblk),
        in_specs=[seq] * 9 + [ec_spec, par, par],
        out_specs=seq,
        out_shape=jax.ShapeDtypeStruct((b, s, w), BF16),
        scratch_shapes=[pltpu.VMEM((nsq, w // gw, gw, gw), F32)],
        compiler_params=_cp("parallel", "arbitrary"),
        name="rwkv_chunk",
    )(*ops, ec4, ln_w.reshape(1, w), ln_b.reshape(1, w))


def _gdn_prep_kernel(x_ref, xp_ref, sm_ref, z_ref, cw_ref, alog_ref, dtb_ref, ltri_ref, utri_ref, same_ref, tot_ref,
                     q_ref, k_ref, kb_ref, vb_ref, kbe_ref, qg_ref, kg_ref, zs_ref, dec_ref, egl_ref):
    first = pl.program_id(1) == 0
    dh = GDN_HEAD_DIM
    w = MIX_W
    x = x_ref[...]
    ts = x.shape[0]
    prev = jnp.where(first, 0.0, xp_ref[...])
    row = lax.broadcasted_iota(jnp.int32, x.shape, 0)
    acc = x * cw_ref[GDN_CONV - 1:GDN_CONV, :]
    for d in range(1, GDN_CONV):
        sh = pltpu.roll(x, d, 0)
        for rr in range(d):
            sh = jnp.where(row == rr, prev[8 - d + rr:8 - d + rr + 1, :], sh)
        acc = acc + sh * cw_ref[GDN_CONV - 1 - d:GDN_CONV - d, :]
    act = acc * _sigmoid(acc)
    sm = sm_ref[...]
    gs = [-jnp.exp(alog_ref[:, h:h + 1]) * _softplus(sm[:, 16 + h:17 + h] + dtb_ref[:, h:h + 1])
          for h in range(GDN_HEADS)]
    gc = jnp.concatenate([jnp.broadcast_to(g, (ts, CHUNK)) for g in gs], axis=1)
    gwide = jnp.concatenate([jnp.broadcast_to(g, (ts, dh)) for g in gs], axis=1)
    gamc = _sum01(ltri_ref[...], gc)
    sufc = _sum01(utri_ref[...], gc)
    t_in = lax.broadcasted_iota(jnp.int32, gc.shape, 0) % CHUNK
    s_in = lax.broadcasted_iota(jnp.int32, gc.shape, 1) % CHUNK
    gamr = _sum01(same_ref[...], jnp.where(t_in == s_in, gamc, 0.0))
    dec_ref[...] = jnp.exp(jnp.where(t_in >= s_in, gamc - gamr, NEG))
    egl_ref[...] = jnp.exp(_sum01(tot_ref[...], gwide))[0:egl_ref.shape[0]]
    for h in range(GDN_HEADS):
        ls = slice(h * dh, (h + 1) * dh)
        q = act[:, h * dh:(h + 1) * dh]
        k = act[:, w + h * dh:w + (h + 1) * dh]
        v = act[:, 2 * w + h * dh:2 * w + (h + 1) * dh]
        q = q * lax.rsqrt(jnp.sum(q * q, axis=-1, keepdims=True) + EPS) * (dh ** -0.5)
        k = k * lax.rsqrt(jnp.sum(k * k, axis=-1, keepdims=True) + EPS)
        beta = _sigmoid(sm[:, 20 + h:21 + h])
        eg = jnp.exp(gamc[:, h * CHUNK:h * CHUNK + 1])
        es = jnp.exp(sufc[:, h * CHUNK:h * CHUNK + 1])
        kb = k * beta
        q_ref[:, ls] = q.astype(BF16)
        k_ref[:, ls] = k.astype(BF16)
        kb_ref[:, ls] = kb.astype(BF16)
        vb_ref[:, ls] = (v * beta).astype(BF16)
        kbe_ref[:, ls] = (kb * eg).astype(BF16)
        qg_ref[:, ls] = (q * eg).astype(BF16)
        kg_ref[:, ls] = (k * es).astype(BF16)
    z = z_ref[...]
    zs_ref[...] = (z * _sigmoid(z)).astype(BF16)


def _gdn_prep(proj3, conv_w, a_log, dt_bias):
    b, s, _ = proj3.shape
    w = MIX_W
    ts = PREP_TILE
    nrb = ts // 8
    nck = ts // CHUNK
    cw = HEADS_PER_PASS * CHUNK
    ltri, utri, same, tot = _chunk_consts(ts)
    full = lambda shp: pl.BlockSpec(shp, lambda bi, i: (0,) * len(shp))
    seq = pl.BlockSpec((None, ts, w), lambda bi, i: (bi, i, 0))
    return pl.pallas_call(
        _gdn_prep_kernel,
        grid=(b, s // ts),
        in_specs=[pl.BlockSpec((None, ts, 3 * w), lambda bi, i: (bi, i, C_GDN // (3 * w))),
                  pl.BlockSpec((None, 8, 3 * w), lambda bi, i: (bi, jnp.maximum(i * nrb - 1, 0), C_GDN // (3 * w))),
                  pl.BlockSpec((None, ts, 128), lambda bi, i: (bi, i, C_SM0 // 128)),
                  pl.BlockSpec((None, ts, w), lambda bi, i: (bi, i, (C_GDN + 3 * w) // w)),
                  full((GDN_CONV, 3 * w)), full((1, GDN_HEADS)), full((1, GDN_HEADS)),
                  full((ts, ts)), full((ts, ts)), full((ts, ts)), full(tot.shape)],
        out_specs=[seq] * 8 + [pl.BlockSpec((None, ts, cw), lambda bi, i: (bi, i, 0)),
                               pl.BlockSpec((None, nck, w), lambda bi, i: (bi, i, 0))],
        out_shape=[jax.ShapeDtypeStruct((b, s, w), BF16)] * 8 + [jax.ShapeDtypeStruct((b, s, cw), F32),
                                                                 jax.ShapeDtypeStruct((b, s // CHUNK, w), F32)],
        compiler_params=_cp("parallel", "parallel"),
        name="gdn_prep",
    )(proj3, proj3, proj3, proj3, conv_w, a_log.reshape(1, GDN_HEADS), dt_bias.reshape(1, GDN_HEADS),
      ltri, utri, same, tot)


def _gdn_chunk_kernel(q_ref, k_ref, kb_ref, vb_ref, kbe_ref, qg_ref, kg_ref, zs_ref, dec_ref, egl_ref, nw_ref,
                      o_ref, s_ref):
    c = CHUNK
    dh = GDN_HEAD_DIM
    w = MIX_W
    pw = 2 * dh
    npair = w // pw
    nseq = q_ref.shape[0]

    @pl.when(pl.program_id(1) == 0)
    def _():
        s_ref[...] = jnp.zeros_like(s_ref)

    _, strict, eye = _cat_tri_masks()
    cw = HEADS_PER_PASS * c
    bd64 = jnp.where(_block_mask(cw, cw, c, c), 1.0, 0.0).astype(BF16)
    bdk = jnp.where(_block_mask(cw, w, c, dh), 1.0, 0.0).astype(BF16)
    bdp_b = _block_mask(pw, pw, dh, dh)

    seqs = range(nseq)
    pairs = [slice(p * pw, (p + 1) * pw) for p in range(npair)]

    def chunk(ci, carry):
        rows = pl.ds(pl.multiple_of(ci * c, c), c)
        lhs = [jnp.concatenate([kb_ref[sq, rows, :], q_ref[sq, rows, :]], axis=0) for sq in seqs]
        aq = [_dot_nt(lhs[sq], _bd_tile(k_ref[sq, rows, :], bdk)) for sq in seqs]
        dec = [dec_ref[sq, rows, :] for sq in seqs]
        p = [-jnp.where(strict, aq[sq][0:c] * dec[sq], 0.0) for sq in seqs]
        t = [jnp.where(eye, 1.0, 0.0) + p[sq] for sq in seqs]
        for _ in range(int(math.log2(c)) - 1):
            pb = [x.astype(BF16) for x in p]
            p = [_dot(pb[sq], _bd_tile(pb[sq], bd64)) for sq in seqs]
            t = [t[sq] + _dot(t[sq].astype(BF16), _bd_tile(p[sq].astype(BF16), bd64)) for sq in seqs]
        tb = [x.astype(BF16) for x in t]
        u = [_dot(tb[sq], _bd_tile(vb_ref[sq, rows, :], bdk)) for sq in seqs]
        wm = [_dot(tb[sq], _bd_tile(kbe_ref[sq, rows, :], bdk)).astype(BF16) for sq in seqs]
        st = [[s_ref[sq, pi] for pi in range(npair)] for sq in seqs]
        stb = [[x.astype(BF16) for x in st[sq]] for sq in seqs]
        ws = [jnp.concatenate([_dot(wm[sq][:, ps], stb[sq][pi]) for pi, ps in enumerate(pairs)], axis=1)
              for sq in seqs]
        vnb = [(u[sq] - ws[sq]).astype(BF16) for sq in seqs]
        qs = [jnp.concatenate([_dot(qg_ref[sq, rows, ps], stb[sq][pi]) for pi, ps in enumerate(pairs)], axis=1)
              for sq in seqs]
        o = [qs[sq] + _dot((aq[sq][c:2 * c] * dec[sq]).astype(BF16), _bd_tile(vnb[sq], bdk)) for sq in seqs]
        upd = [[_dot_tn(kg_ref[sq, rows, ps], vnb[sq][:, ps]) for ps in pairs] for sq in seqs]
        for sq in seqs:
            egl = egl_ref[sq, pl.ds(ci, 1), :]
            for pi, ps in enumerate(pairs):
                s_ref[sq, pi] = st[sq][pi] * egl[:, ps] + jnp.where(bdp_b, upd[sq][pi], 0.0)
            for h in range(GDN_HEADS):
                ls = slice(h * dh, (h + 1) * dh)
                oh = o[sq][:, ls]
                ms = jnp.mean(oh * oh, axis=-1, keepdims=True)
                o_ref[sq, rows, ls] = (oh * lax.rsqrt(ms + EPS) * nw_ref[...]
                                       * zs_ref[sq, rows, ls].astype(F32)).astype(o_ref.dtype)
        return carry

    lax.fori_loop(0, q_ref.shape[1] // c, chunk, 0)


def _gdn_chunk(ops, dec, egl, norm_w):
    b, s, w = ops[0].shape
    dh = GDN_HEAD_DIM
    nsq = _seq_per_step(b)
    lblk = REC_BLOCK
    seq = pl.BlockSpec((nsq, lblk, w), lambda bi, i: (bi, i, 0))
    egl, egl_spec = _per_chunk_spec(egl, nsq)
    return pl.pallas_call(
        _gdn_chunk_kernel,
        grid=(b // nsq, s // lblk),
        in_specs=[seq] * 8 + [pl.BlockSpec((nsq, lblk, dec.shape[2]), lambda bi, i: (bi, i, 0)), egl_spec,
                              pl.BlockSpec((1, dh), lambda bi, i: (0, 0))],
        out_specs=seq,
        out_shape=jax.ShapeDtypeStruct((b, s, w), BF16),
        scratch_shapes=[pltpu.VMEM((nsq, w // (2 * dh), 2 * dh, 2 * dh), F32)],
        compiler_params=_cp("parallel", "arbitrary"),
        name="gdn_chunk",
    )(*ops, dec, egl, norm_w.reshape(1, dh))


def _merge_kernel(x_ref, oa_ref, ob_ref, oc_ref, ga_ref, gb_ref, gc_ref, wb_ref, wo_ref, o_ref):
    merged = None
    for j, (br, gate) in enumerate(((oa_ref, ga_ref), (ob_ref, gb_ref), (oc_ref, gc_ref))):
        t = _sigmoid(gate[...]) * _dot(br[...], wb_ref[j])
        merged = t if merged is None else merged + t
    o_ref[...] = x_ref[...] + _dot(merged.astype(BF16), wo_ref[...])


def _merge(x2, o_a, o_b, o_c, proj, w_branch, w_out, tm=512):
    t, d = x2.shape
    w = MIX_W
    tm = min(tm, t)
    br = pl.BlockSpec((tm, w), lambda i: (i, 0))
    gate = lambda j: pl.BlockSpec((tm, d), lambda i: (i, C_GATE // d + j))
    return pl.pallas_call(
        _merge_kernel,
        grid=(t // tm,),
        in_specs=[pl.BlockSpec((tm, d), lambda i: (i, 0)), br, br, br, gate(0), gate(1), gate(2),
                  pl.BlockSpec((3, w, d), lambda i: (0, 0, 0)),
                  pl.BlockSpec((d, d), lambda i: (0, 0))],
        out_specs=pl.BlockSpec((tm, d), lambda i: (i, 0)),
        out_shape=jax.ShapeDtypeStruct((t, d), F32),
        compiler_params=_cp("parallel"),
        name="merge",
    )(x2, o_a.reshape(t, w), o_b.reshape(t, w), o_c.reshape(t, w), proj, proj, proj,
      w_branch.astype(BF16), w_out.astype(BF16))


def _ffn_kernel(x_ref, g_ref, w1_ref, w2_ref, o_ref, hn_ref):
    j = pl.program_id(1)

    @pl.when(j == 0)
    def _():
        x = x_ref[...]
        ms = jnp.mean(x * x, axis=-1, keepdims=True)
        hn_ref[...] = (x * lax.rsqrt(ms + EPS) * g_ref[...]).astype(BF16)
        o_ref[...] = x

    h1 = jnp.maximum(_dot(hn_ref[...], w1_ref[...]), 0.0)
    o_ref[...] += _dot((h1 * h1).astype(BF16), w2_ref[...])


def _ffn(x2, g, w1, w2, tm=1024, tf=512):
    t, d = x2.shape
    f = w1.shape[1]
    tm = min(tm, t)
    return pl.pallas_call(
        _ffn_kernel,
        grid=(t // tm, f // tf),
        in_specs=[pl.BlockSpec((tm, d), lambda i, j: (i, 0)),
                  pl.BlockSpec((1, d), lambda i, j: (0, 0)),
                  pl.BlockSpec((d, tf), lambda i, j: (0, j)),
                  pl.BlockSpec((tf, d), lambda i, j: (j, 0))],
        out_specs=pl.BlockSpec((tm, d), lambda i, j: (i, 0)),
        out_shape=jax.ShapeDtypeStruct((t, d), F32),
        scratch_shapes=[pltpu.VMEM((tm, d), BF16)],
        compiler_params=_cp("parallel", "arbitrary"),
        name="ffn",
    )(x2, g.reshape(1, d), w1.astype(BF16), w2.astype(BF16))


def kernel(x, norm_mix_g, w_in, nsa_q_norm, nsa_k_norm, nsa_cmp_pos, nsa_cmp_w1, nsa_cmp_w2, rwkv_mu, rwkv_w0, rwkv_w_up, rwkv_a0, rwkv_a_up, rwkv_g_up, rwkv_k_k, rwkv_k_a, rwkv_r_k, rwkv_ln_w, rwkv_ln_b, rwkv_v0, rwkv_vres_down, rwkv_vres_up, gdn_conv_w, gdn_a_log, gdn_dt_bias, gdn_norm_w, w_branch, w_out, norm_ffn_g, w_ff1, w_ff2):
    b, s, d = x.shape
    depth = w_in.shape[0]
    perm = jnp.asarray(_proj_perm())
    x2 = x.reshape(b * s, d)
    v_first = None
    for i in range(depth):
        vd = rwkv_vres_down[i - 1] if i > 0 else jnp.zeros((d, 32), F32)
        w_ext = jnp.concatenate([w_in[i], vd, jnp.zeros((d, 1), F32)], axis=1)
        w_all = jnp.take(w_ext, perm, axis=1).astype(BF16)
        proj = _rms_matmul(x2, norm_mix_g[i], w_all)
        proj3 = proj.reshape(b, s, N_PROJ)

        o_a = _nsa(proj3, nsa_q_norm[i], nsa_k_norm[i], nsa_cmp_pos[i], nsa_cmp_w1[i], nsa_cmp_w2[i])

        vres = None if i == 0 else (rwkv_v0[i - 1], rwkv_vres_up[i - 1])
        rw = _rwkv_prep(proj3, rwkv_mu[i], rwkv_w0[i], rwkv_w_up[i], rwkv_a0[i], rwkv_a_up[i], rwkv_g_up[i],
                        rwkv_k_k[i], rwkv_k_a[i], rwkv_r_k[i], v_first, vres)
        if i == 0:
            v_first = rw[10]
        o_b = _rwkv_chunk(rw[:9], rw[9], rwkv_ln_w[i], rwkv_ln_b[i])

        gd = _gdn_prep(proj3, gdn_conv_w[i], gdn_a_log[i], gdn_dt_bias[i])
        o_c = _gdn_chunk(gd[:8], gd[8], gd[9], gdn_norm_w[i])

        x2 = _merge(x2, o_a, o_b, o_c, proj, w_branch[i], w_out[i])
        x2 = _ffn(x2, norm_ffn_g[i], w_ff1[i], w_ff2[i])
    return x2.reshape(b, s, d)
```

```python
import functools
import math

import numpy as np
import jax
import jax.numpy as jnp
from jax import lax
from jax.experimental import pallas as pl
from jax.experimental.pallas import tpu as pltpu

F32 = jnp.float32
BF16 = jnp.bfloat16

D_MODEL = 1024
MIX_W = 512
NSA_HEAD_DIM = 64
NSA_KV_GROUPS = 2
NSA_HPG = 4
NSA_CMP_BLOCK = 32
NSA_CMP_STRIDE = 16
NSA_CMP_HIDDEN = 256
NSA_SEL_BLOCK = 64
NSA_N_SEL = 16
NSA_WINDOW = 512
Q_BLOCK = 128
NSA_IN = 1304
RWKV_HEADS = 8
RWKV_HEAD_DIM = 64
RWKV_LN_EPS = 64e-5
RWKV_IN = 1792
GDN_HEADS = 4
GDN_HEAD_DIM = 128
GDN_CONV = 4
GDN_IN = 2056
D_IN = 8224
D_FF = 4096
EPS = 1e-6
NEG = -1e30
FORCE = 1e4

C_KV = 0
C_Q = 768
C_SM0 = 1280
C_SM1 = 1408
C_RKV = 1536
C_GDN = 3072
C_GATE = 5120
C_LR = 8192
N_PROJ = 8448
VMEM_LIMIT = 48 * 1024 * 1024


def _proj_perm():
    zero = D_IN + 32
    perm = np.full((N_PROJ,), zero, np.int32)
    perm[C_KV:C_KV + 768] = np.arange(512, 1280)
    perm[C_Q:C_Q + 512] = np.arange(0, 512)
    perm[C_SM0:C_SM0 + 12] = 1280 + np.arange(12)
    perm[C_SM1:C_SM1 + 12] = 1292 + np.arange(12)
    rw = NSA_IN
    gd = NSA_IN + RWKV_IN
    gt = gd + GDN_IN
    perm[C_SM0 + 16:C_SM0 + 24] = gd + 2048 + np.arange(8)
    perm[C_SM0 + 32:C_SM0 + 64] = D_IN + np.arange(32)
    perm[C_RKV:C_RKV + 1536] = rw + np.arange(1536)
    perm[C_LR:C_LR + 256] = rw + 1536 + np.arange(256)
    perm[C_GDN:C_GDN + 2048] = gd + np.arange(2048)
    perm[C_GATE:C_GATE + 3072] = gt + np.arange(3072)
    return perm


def _cp(*sem):
    return pltpu.CompilerParams(dimension_semantics=sem, vmem_limit_bytes=VMEM_LIMIT)


def _dot(a, b):
    return jnp.dot(a, b, preferred_element_type=F32)


def _dot_nt(a, b):
    return lax.dot_general(a, b, (((1,), (1,)), ((), ())), preferred_element_type=F32)


def _dot_tn(a, b):
    return lax.dot_general(a, b, (((0,), (0,)), ((), ())), preferred_element_type=F32)


def _split(a):
    hi = a.astype(BF16)
    lo = (a - hi.astype(F32)).astype(BF16)
    return hi, lo


def _mm3(a, b):
    ah, al = _split(a)
    bh, bl = _split(b)
    return _dot(ah, bh) + (_dot(ah, bl) + _dot(al, bh))


def _split3(x):
    hi = x.astype(BF16)
    r1 = x - hi.astype(F32)
    mid = r1.astype(BF16)
    return hi, mid, (r1 - mid.astype(F32)).astype(BF16)


def _sum01(m, x):
    hi, mid, lo = x if isinstance(x, tuple) else _split3(x)
    return _dot(m, hi) + (_dot(m, mid) + _dot(m, lo))


def _segsum_left(m, x):
    hi, lo = _split(x)
    return _dot(m, hi) + _dot(m, lo)


def _segsum(x, seg):
    hi, lo = _split(x)
    return _dot(hi, seg) + _dot(lo, seg)


CHUNK = 64
HEADS_PER_PASS = 4


def _chunk_consts(ts):
    r = np.arange(ts)
    same = (r[:, None] // CHUNK) == (r[None, :] // CHUNK)
    ltri = same & (r[:, None] >= r[None, :])
    utri = same & (r[:, None] < r[None, :])
    nck = ts // CHUNK
    tot = np.zeros((max(nck, 8), ts), bool)
    tot[:nck] = (r[None, :] // CHUNK) == np.arange(nck)[:, None]
    return [jnp.asarray(m, BF16) for m in (ltri, utri, same, tot)]


def _block_mask(rows, cols, rblk, cblk):
    r = lax.broadcasted_iota(jnp.int32, (rows, cols), 0) // rblk
    c = lax.broadcasted_iota(jnp.int32, (rows, cols), 1) // cblk
    return r == c


def _cat_tri_masks():
    w = HEADS_PER_PASS * CHUNK
    t = lax.broadcasted_iota(jnp.int32, (CHUNK, w), 0)
    s = lax.broadcasted_iota(jnp.int32, (CHUNK, w), 1) % CHUNK
    return t >= s, t > s, t == s


def _bd_tile(y, bd):
    return jnp.concatenate([y] * HEADS_PER_PASS, axis=0) * bd


def _softplus(z):
    return jnp.maximum(z, 0.0) + jnp.log1p(jnp.exp(-jnp.abs(z)))


def _sigmoid(z):
    return 1.0 / (1.0 + jnp.exp(-z))


def _rms_matmul_kernel(x_ref, g_ref, w_ref, o_ref, hn_ref):
    @pl.when(pl.program_id(1) == 0)
    def _():
        x = x_ref[...]
        ms = jnp.mean(x * x, axis=-1, keepdims=True)
        hn_ref[...] = (x * lax.rsqrt(ms + EPS) * g_ref[...]).astype(BF16)

    o_ref[...] = _dot(hn_ref[...], w_ref[...]).astype(o_ref.dtype)


def _rms_matmul(x2, g, w, tm=1024, tn=768):
    t, d = x2.shape
    n = w.shape[1]
    tm = min(tm, t)
    return pl.pallas_call(
        _rms_matmul_kernel,
        grid=(t // tm, n // tn),
        in_specs=[pl.BlockSpec((tm, d), lambda i, j: (i, 0)),
                  pl.BlockSpec((1, d), lambda i, j: (0, 0)),
                  pl.BlockSpec((d, tn), lambda i, j: (0, j))],
        out_specs=pl.BlockSpec((tm, tn), lambda i, j: (i, j)),
        out_shape=jax.ShapeDtypeStruct((t, n), BF16),
        scratch_shapes=[pltpu.VMEM((tm, d), BF16)],
        compiler_params=_cp("parallel", "arbitrary"),
        name="rms_proj",
    )(x2, g.reshape(1, d), w)


def _nsa_kvprep_kernel(kv_ref, kg_ref, kc_ref, vc_ref, ks_ref, vs_ref, kw_ref, vw_ref):
    outs = (kc_ref, vc_ref, ks_ref, vs_ref, kw_ref, vw_ref)
    for j in range(6):
        for g in range(NSA_KV_GROUPS):
            lo = j * 128 + g * NSA_HEAD_DIM
            piece = kv_ref[:, lo:lo + NSA_HEAD_DIM].astype(F32)
            if j in (2, 4):
                gain = kg_ref[j // 2:j // 2 + 1, :]
                ms = jnp.mean(piece * piece, axis=-1, keepdims=True)
                piece = piece * lax.rsqrt(ms + EPS) * gain
            outs[j][g] = piece.astype(outs[j].dtype)


def _nsa_kvprep(proj3, k_g, ts=1024):
    b, s, _ = proj3.shape
    g, dh = NSA_KV_GROUPS, NSA_HEAD_DIM
    out_spec = pl.BlockSpec((None, g, ts, dh), lambda bi, i: (bi, 0, i, 0))
    shp = lambda dt: jax.ShapeDtypeStruct((b, g, s, dh), dt)
    return pl.pallas_call(
        _nsa_kvprep_kernel,
        grid=(b, s // ts),
        in_specs=[pl.BlockSpec((None, ts, 768), lambda bi, i: (bi, i, C_KV // 768)),
                  pl.BlockSpec((3, dh), lambda bi, i: (0, 0))],
        out_specs=[out_spec] * 6,
        out_shape=[shp(F32), shp(F32), shp(BF16), shp(BF16), shp(BF16), shp(BF16)],
        compiler_params=_cp("parallel", "parallel"),
        name="nsa_kvprep",
    )(proj3, k_g)


def _nsa_compress_kernel(x_ref, w1_ref, pos_ref, w2_ref, g_ref, o_ref, *, normalize):
    x = x_ref[...].astype(BF16)
    half = x.shape[1]
    a = _dot(x, w1_ref[0:half, :])
    bm = _dot(x, w1_ref[half:2 * half, :])
    bias = _dot(pos_ref[...].astype(BF16), w1_ref[...])[0:1, :]
    n = x.shape[0]
    hid = a + pltpu.roll(bm, n - 1, 0) + bias
    hid = hid * _sigmoid(hid)
    out = _dot(hid.astype(BF16), w2_ref[...])
    if normalize:
        ms = jnp.mean(out * out, axis=-1, keepdims=True)
        out = out * lax.rsqrt(ms + EPS) * g_ref[...]
    o_ref[...] = out.astype(o_ref.dtype)


def _nsa_compress(xh, w1, pos, w2, gain, normalize):
    bg, n, half = xh.shape
    dh = NSA_HEAD_DIM
    pos8 = jnp.broadcast_to(pos.reshape(1, 2 * half), (8, 2 * half))
    return pl.pallas_call(
        functools.partial(_nsa_compress_kernel, normalize=normalize),
        grid=(bg,),
        in_specs=[pl.BlockSpec((None, n, half), lambda i: (i, 0, 0)),
                  pl.BlockSpec((2 * half, NSA_CMP_HIDDEN), lambda i: (0, 0)),
                  pl.BlockSpec((8, 2 * half), lambda i: (0, 0)),
                  pl.BlockSpec((NSA_CMP_HIDDEN, dh), lambda i: (0, 0)),
                  pl.BlockSpec((1, dh), lambda i: (0, 0))],
        out_specs=pl.BlockSpec((None, n, dh), lambda i: (i, 0, 0)),
        out_shape=jax.ShapeDtypeStruct((bg, n, dh), BF16),
        compiler_params=_cp("parallel"),
        name="nsa_compress",
    )(xh, w1.astype(BF16), pos8, w2.astype(BF16), gain.reshape(1, dh))


SEL_TILE = 256
LOG2E = 1.4426950408889634


def _softmax_cols(s, valid):
    s = jnp.where(valid, s, NEG)
    p = jnp.exp2(s - jnp.max(s, axis=0, keepdims=True))
    return p, jnp.sum(p, axis=0, keepdims=True)


def _nsa_attn_t_kernel(q_ref, sm_ref, kc_ref, vct_ref, ks_ref, vst_ref, kw_ref, vwt_ref, c2s_ref, qg_ref, o_ref,
                       sb_ref, sa_ref, sb2_ref, pa_ref, pb_ref, acc_ref, al_ref, m_ref, l_ref, *, n_sel):
    dh, hg, qb = NSA_HEAD_DIM, NSA_HPG, Q_BLOCK
    cols = hg * qb
    tk = SEL_TILE
    blk = pl.program_id(2)
    start = blk * qb

    xt = jnp.transpose(q_ref[...].astype(F32))
    qs = []
    for h in range(hg):
        xh = xt[h * dh:(h + 1) * dh, :]
        ms = jnp.mean(xh * xh, axis=0, keepdims=True)
        qs.append(xh * lax.rsqrt(ms + EPS) * qg_ref[...])
    qt = jnp.concatenate(qs, axis=1).astype(BF16)
    tq_l =start + lax.broadcasted_iota(jnp.int32, (1, qb), 1)
    heads = [slice(h * qb, (h + 1) * qb) for h in range(hg)]

    t_col = jnp.concatenate([tq_l] * hg, axis=1)

    n_cmp = kc_ref.shape[0]
    cvalid = (lax.broadcasted_iota(jnp.int32, (n_cmp, 1), 0) * NSA_CMP_STRIDE + (NSA_CMP_BLOCK - 1)) <= t_col
    p, l = _softmax_cols(_dot(kc_ref[...], qt), cvalid)
    p = p * jnp.where(t_col >= NSA_CMP_BLOCK - 1, 1.0 / l, 0.0)
    o_c = _dot(vct_ref[...], p.astype(BF16))
    psum = p[:, heads[0]]
    for cs in heads[1:]:
        psum = psum + p[:, cs]
    imp_t = _segsum_left(c2s_ref[...], psum)

    n_slc = imp_t.shape[0]
    jr = lax.broadcasted_iota(jnp.int32, (n_slc, 1), 0)
    cur = jnp.right_shift(tq_l, int(math.log2(NSA_SEL_BLOCK)))
    forced = (jr == 0) | (jr == cur) | (jr == cur - 1)
    causal = jr * NSA_SEL_BLOCK <= tq_l
    val = jnp.where(forced, -jnp.inf, jnp.where(causal, imp_t, -FORCE))
    jrf = jr.astype(F32)
    bias = jnp.where(forced, 0.0, NEG)

    def select_round(val, bias):
        mx = jnp.max(val, axis=0, keepdims=True)
        idx = jnp.min(jnp.where(val == mx, jrf, float(n_slc)), axis=0, keepdims=True)
        hit = jrf == idx
        return jnp.where(hit, -jnp.inf, val), jnp.where(hit, 0.0, bias)

    wlen = NSA_WINDOW + qb
    base = pl.multiple_of(jnp.maximum(start - NSA_WINDOW, 0), qb)
    dist = t_col - (base + lax.broadcasted_iota(jnp.int32, (wlen, 1), 0))
    p, l = _softmax_cols(_dot(kw_ref[pl.ds(base, wlen), :], qt), (dist >= 0) & (dist < NSA_WINDOW))
    o_w = _dot(vwt_ref[:, pl.ds(base, wlen)], p.astype(BF16)) * (1.0 / l)

    for _ in range(n_sel - 3):
        val, bias = select_round(val, bias)
    sb_ref[...] = bias

    bpt = tk // NSA_SEL_BLOCK
    last = start // tk
    n_pairs = (last + 1) // 2

    def key_off(i):
        return pl.multiple_of(jnp.minimum(i, last) * tk, tk)

    def scores(i):
        return _dot(ks_ref[pl.ds(key_off(i), tk), :], qt)

    def step(tile, prev_tile, next_tile, s_cur, s_nxt, p_cur, p_prev, first):
        s_nxt[...] = scores(next_tile)
        if not first:
            pv = _dot(vst_ref[:, pl.ds(key_off(prev_tile), tk)], p_prev[...])
        rows = [sb_ref[pl.ds(jnp.minimum(tile, last) * bpt + j, 1), :] for j in range(bpt)]
        if first:
            visible = (tile * tk + lax.broadcasted_iota(jnp.int32, (tk, 1), 0)) <= tq_l
        else:
            rows = [jnp.where(tile < last, r, NEG) for r in rows]
        for cs in heads:
            sh = jnp.concatenate([s_cur[j * NSA_SEL_BLOCK:(j + 1) * NSA_SEL_BLOCK, cs] + rows[j]
                                  for j in range(bpt)], axis=0)
            if first:
                sh = jnp.where(visible, sh, NEG)
            m_old = m_ref[:, cs]
            m_new = jnp.maximum(m_old, jnp.max(sh, axis=0, keepdims=True))
            alpha = jnp.exp2(m_old - m_new)
            p = jnp.exp2(sh - m_new)
            p_cur[:, cs] = p.astype(BF16)
            m_ref[:, cs] = m_new
            l_ref[:, cs] = alpha * l_ref[:, cs] + jnp.sum(p, axis=0, keepdims=True)
            if not first:
                acc_ref[:, cs] = al_ref[:, cs] * acc_ref[:, cs] + pv[:, cs]
            al_ref[:, cs] = alpha

    m_ref[...] = jnp.full((1, cols), NEG, F32)
    l_ref[...] = jnp.zeros((1, cols), F32)
    acc_ref[...] = jnp.zeros((dh, cols), F32)
    sa_ref[...] = scores(last)
    step(last, last, 0, sa_ref, sb2_ref, pa_ref, pb_ref, True)

    def sel_body(j, carry):
        t0 = 2 * j
        step(t0, jnp.where(j == 0, last, t0 - 1), t0 + 1, sb2_ref, sa_ref, pb_ref, pa_ref, False)
        step(t0 + 1, t0, t0 + 2, sa_ref, sb2_ref, pa_ref, pb_ref, False)
        return carry

    lax.fori_loop(0, n_pairs, sel_body, 0)
    prev = jnp.where(n_pairs == 0, last, 2 * n_pairs - 1)
    acc = al_ref[...] * acc_ref[...] + _dot(vst_ref[:, pl.ds(key_off(prev), tk)], pa_ref[...])
    o_s = acc * (1.0 / l_ref[...])

    gt = jnp.transpose(_sigmoid(sm_ref[...].astype(F32)))
    outs = []
    for h, cs in enumerate(heads):
        outs.append(gt[3 * h:3 * h + 1, :] * o_c[:, cs] + gt[3 * h + 1:3 * h + 2, :] * o_s[:, cs]
                    + gt[3 * h + 2:3 * h + 3, :] * o_w[:, cs])
    o_ref[...] = jnp.transpose(jnp.concatenate(outs, axis=0)).astype(o_ref.dtype)


def _nsa(proj3, q_g, k_g, cmp_pos, cmp_w1, cmp_w2):
    b, s, _ = proj3.shape
    g, dh, qb = NSA_KV_GROUPS, NSA_HEAD_DIM, Q_BLOCK
    kc_raw, vc_raw, ks, vs, kw, vw = _nsa_kvprep(proj3, k_g, ts=min(1024, s))
    n_half = s // NSA_CMP_STRIDE
    half = NSA_CMP_STRIDE * dh
    kc = _nsa_compress(kc_raw.reshape(b * g, n_half, half), cmp_w1[0], cmp_pos[0], cmp_w2[0], k_g[0], True)
    vc = _nsa_compress(vc_raw.reshape(b * g, n_half, half), cmp_w1[1], cmp_pos[1], cmp_w2[1], k_g[0], False)
    kc = kc.reshape(b, g, n_half, dh)
    vc = vc.reshape(b, g, n_half, dh)

    n_slc = s // NSA_SEL_BLOCK
    n_sel = min(NSA_N_SEL, n_slc)
    cmp_start = np.arange(n_half) * NSA_CMP_STRIDE
    slc_start = np.arange(n_slc) * NSA_SEL_BLOCK
    overlap = np.clip(np.minimum(cmp_start[:, None] + NSA_CMP_BLOCK, slc_start[None, :] + NSA_SEL_BLOCK)
                      - np.maximum(cmp_start[:, None], slc_start[None, :]), 0, None)
    c2s = (overlap / NSA_CMP_BLOCK).T

    k_spec = lambda n: pl.BlockSpec((None, None, n, dh), lambda bi, gi, i: (bi, gi, 0, 0))
    vt_spec = lambda n: pl.BlockSpec((None, None, dh, n), lambda bi, gi, i: (bi, gi, 0, 0))
    tr = lambda v: jnp.swapaxes(v, 2, 3)
    cols = NSA_HPG * qb
    q_gain = jnp.broadcast_to((q_g * (dh ** -0.5 * LOG2E)).reshape(dh, 1), (dh, qb))
    return pl.pallas_call(
        functools.partial(_nsa_attn_t_kernel, n_sel=n_sel),
        grid=(b, g, s // qb),
        in_specs=[pl.BlockSpec((None, qb, 256), lambda bi, gi, i: (bi, i, C_Q // 256 + gi)),
                  pl.BlockSpec((None, qb, 128), lambda bi, gi, i: (bi, i, C_SM0 // 128 + gi)),
                  k_spec(n_half), vt_spec(n_half), k_spec(s), vt_spec(s), k_spec(s), vt_spec(s),
                  pl.BlockSpec((n_slc, n_half), lambda bi, gi, i: (0, 0)),
                  pl.BlockSpec((dh, qb), lambda bi, gi, i: (0, 0))],
        out_specs=pl.BlockSpec((None, qb, 256), lambda bi, gi, i: (bi, i, gi)),
        out_shape=jax.ShapeDtypeStruct((b, s, MIX_W), BF16),
        scratch_shapes=[pltpu.VMEM((n_slc, qb), F32), pltpu.VMEM((SEL_TILE, cols), F32),
                        pltpu.VMEM((SEL_TILE, cols), F32), pltpu.VMEM((SEL_TILE, cols), BF16),
                        pltpu.VMEM((SEL_TILE, cols), BF16), pltpu.VMEM((dh, cols), F32),
                        pltpu.VMEM((1, cols), F32), pltpu.VMEM((1, cols), F32), pltpu.VMEM((1, cols), F32)],
        compiler_params=_cp("parallel", "parallel", "arbitrary"),
        name="nsa_attn",
    )(proj3, proj3, kc, tr(vc), ks, tr(vs), kw, tr(vw), jnp.asarray(c2s, BF16), q_gain)


def _shift_rows(x, prev_row, first):
    rolled = pltpu.roll(x, 1, 0)
    row = lax.broadcasted_iota(jnp.int32, x.shape, 0)
    prev = jnp.where(first, 0.0, prev_row)
    return jnp.where(row == 0, prev, rolled)


PREP_TILE = 512
PREV_ROWS = 16


def _rwkv_prep_kernel(*refs, has_vres):
    (rkv_ref, rkvp_ref, lr_ref, lrp_ref, mu1_ref, mu2_ref, w0_ref, wup_ref, a0_ref, aup_ref, gup_ref,
     kkp_ref, kap_ref, rkp_ref, seg_ref, ltri_ref, utri_ref, tot_ref) = refs[:18]
    if has_vres:
        sm_ref, vf_ref, v0_ref, vu_ref = refs[18:22]
        outs = refs[22:]
    else:
        outs = refs[18:]
    kapo_ref, beto_ref, kto_ref, rto_ref, ktc_ref, betc_ref, vo_ref, bon_ref, g_ref, ec_ref = outs[:10]
    first = pl.program_id(1) == 0
    w = MIX_W
    last_prev = slice(PREV_ROWS - 1, PREV_ROWS)
    c = rkv_ref[...].astype(F32)
    c = c + (_shift_rows(c, rkvp_ref[last_prev, :].astype(F32), first) - c) * mu1_ref[...]
    lr = lr_ref[...].astype(F32)
    lr = lr + (_shift_rows(lr, lrp_ref[last_prev, :].astype(F32), first) - lr) * mu2_ref[...]
    r, k, v = c[:, 0:w], c[:, w:2 * w], c[:, 2 * w:3 * w]
    wd, ad, gd = lr[:, 0:64], lr[:, 64:128], lr[:, 128:256]
    wlog = -_softplus(-(w0_ref[...] + _mm3(jnp.tanh(wd), wup_ref[...]))) - 0.5
    lw = -jnp.exp(wlog)
    a = _sigmoid(a0_ref[...] + _mm3(ad, aup_ref[...]))
    g_ref[...] = _mm3(_sigmoid(gd), gup_ref[...]).astype(g_ref.dtype)
    if has_vres:
        mix = _sigmoid(v0_ref[...] + _mm3(sm_ref[:, 32:64].astype(F32), vu_ref[...]))
        v = v + (vf_ref[...] - v) * mix
    else:
        outs[10][...] = v
    seg = seg_ref[...]
    kk = k * kkp_ref[...]
    kk = kk * lax.rsqrt(_segsum(kk * kk, seg) + EPS)
    k = k * (1.0 + (a - 1.0) * kap_ref[...])
    kka = kk * a
    lw3 = _split3(lw)
    lcum = _sum01(ltri_ref[...], lw3)
    e_suf = jnp.exp(_sum01(utri_ref[...], lw3))
    e_inv = jnp.exp(-lcum)
    kapo_ref[...] = (kk * jnp.exp(lcum - lw)).astype(BF16)
    beto_ref[...] = (kka * e_inv).astype(BF16)
    kto_ref[...] = (k * e_inv).astype(BF16)
    rto_ref[...] = (r * jnp.exp(lcum)).astype(BF16)
    ktc_ref[...] = (k * e_suf).astype(BF16)
    betc_ref[...] = (kka * e_suf).astype(BF16)
    vo_ref[...] = v.astype(BF16)
    bon_ref[...] = (_segsum(r * k * rkp_ref[...], seg) * v).astype(BF16)
    ec_ref[...] = jnp.exp(_sum01(tot_ref[...], lw3))[0:ec_ref.shape[0]]


def _rwkv_prep(proj3, mu, w0, w_up, a0, a_up, g_up, k_k, k_a, r_k, v_first, vres):
    b, s, _ = proj3.shape
    w = MIX_W
    has_vres = vres is not None
    ts = PREP_TILE
    nrb = ts // PREV_ROWS
    nck = ts // CHUNK
    cur = lambda wd, cb: pl.BlockSpec((None, ts, wd), lambda bi, i: (bi, i, cb))
    prev = lambda wd, cb: pl.BlockSpec((None, PREV_ROWS, wd), lambda bi, i: (bi, jnp.maximum(i * nrb - 1, 0), cb))
    full = lambda shp: pl.BlockSpec(shp, lambda bi, i: (0,) * len(shp))
    ltri, utri, _, tot = _chunk_consts(ts)
    hd = np.arange(w) // RWKV_HEAD_DIM
    seg = jnp.asarray(hd[:, None] == hd[None, :], BF16)
    in_specs = [cur(1536, C_RKV // 1536), prev(1536, C_RKV // 1536), cur(256, C_LR // 256), prev(256, C_LR // 256),
                full((1, 1536)), full((1, 256)), full((1, w)), full((64, w)), full((1, w)), full((64, w)),
                full((128, w)), full((1, w)), full((1, w)), full((1, w)), full((w, w)), full((ts, ts)),
                full((ts, ts)), full(tot.shape)]
    args = [proj3, proj3, proj3, proj3, mu[:1536].reshape(1, 1536), mu[1536:].reshape(1, 256), w0.reshape(1, w),
            w_up, a0.reshape(1, w), a_up, g_up, k_k.reshape(1, w), k_a.reshape(1, w), r_k.reshape(1, w),
            seg, ltri, utri, tot]
    if has_vres:
        v0, vu = vres
        in_specs += [cur(128, C_SM0 // 128), pl.BlockSpec((None, ts, w), lambda bi, i: (bi, i, 0)),
                     full((1, w)), full((32, w))]
        args += [proj3, v_first, v0.reshape(1, w), vu]
    seq = pl.BlockSpec((None, ts, w), lambda bi, i: (bi, i, 0))
    out_specs = [seq] * 9 + [pl.BlockSpec((None, nck, w), lambda bi, i: (bi, i, 0))]
    out_shape = [jax.ShapeDtypeStruct((b, s, w), BF16)] * 9 + [jax.ShapeDtypeStruct((b, s // CHUNK, w), F32)]
    if not has_vres:
        out_specs.append(seq)
        out_shape.append(jax.ShapeDtypeStruct((b, s, w), F32))
    return pl.pallas_call(
        functools.partial(_rwkv_prep_kernel, has_vres=has_vres),
        grid=(b, s // ts),
        in_specs=in_specs,
        out_specs=out_specs,
        out_shape=out_shape,
        compiler_params=_cp("parallel", "parallel"),
        name="rwkv_prep",
    )(*args)


def _rwkv_chunk_kernel(kap_ref, bet_ref, kt_ref, rt_ref, ktc_ref, betc_ref, v_ref, bon_ref, g_ref, ec_ref,
                       lnw_ref, lnb_ref, o_ref, s_ref):
    c = CHUNK
    gw = HEADS_PER_PASS * RWKV_HEAD_DIM
    nseq = kap_ref.shape[0]
    ngrp = kap_ref.shape[2] // gw

    @pl.when(pl.program_id(1) == 0)
    def _():
        s_ref[...] = jnp.zeros_like(s_ref)

    incl, strict, eye = _cat_tri_masks()
    bd_b = _block_mask(gw, gw, RWKV_HEAD_DIM, RWKV_HEAD_DIM)
    bd = jnp.where(bd_b, 1.0, 0.0).astype(BF16)

    chains = [(sq, gi) for sq in range(nseq) for gi in range(ngrp)]
    lanes = [slice(gi * gw, (gi + 1) * gw) for _, gi in chains]
    every = range(len(chains))

    def chunk(ci, carry):
        rows = pl.ds(pl.multiple_of(ci * c, c), c)
        at = lambda ref, n: ref[chains[n][0], rows, lanes[n]]
        v = [at(v_ref, n) for n in every]
        lhs = [jnp.concatenate([at(kap_ref, n), at(rt_ref, n)], axis=0) for n in every]
        ab = [_dot_nt(lhs[n], _bd_tile(at(bet_ref, n), bd)) for n in every]
        ak = [_dot_nt(lhs[n], _bd_tile(at(kt_ref, n), bd)) for n in every]
        p = [-jnp.where(strict, ab[n][0:c], 0.0) for n in every]
        t = [jnp.where(eye, 1.0, 0.0) + p[n] for n in every]
        for _ in range(int(math.log2(c)) - 1):
            pb = [x.astype(BF16) for x in p]
            p = [_dot(pb[n], _bd_tile(pb[n], bd)) for n in every]
            t = [t[n] + _dot(t[n].astype(BF16), _bd_tile(p[n].astype(BF16), bd)) for n in every]
        st = [s_ref[sq, gi] for sq, gi in chains]
        kr = [_dot_nt(lhs[n], st[n].astype(BF16)) for n in every]
        vbd = [_bd_tile(v[n], bd) for n in every]
        x = [kr[n][0:c] + _dot(jnp.where(strict, ak[n][0:c], 0.0).astype(BF16), vbd[n]) for n in every]
        ub = [_dot(t[n].astype(BF16), _bd_tile(x[n].astype(BF16), bd)).astype(BF16) for n in every]
        y = [kr[n][c:2 * c] + _dot(jnp.where(incl, ak[n][c:2 * c], 0.0).astype(BF16), vbd[n])
             - _dot(jnp.where(incl, ab[n][c:2 * c], 0.0).astype(BF16), _bd_tile(ub[n], bd)) for n in every]
        upd = [_dot_tn(jnp.concatenate([v[n], ub[n]], axis=0),
                       jnp.concatenate([at(ktc_ref, n), -at(betc_ref, n)], axis=0)) for n in every]
        mom = [_segsum(jnp.concatenate([y[n], y[n] * y[n]], axis=0), bd) * (1.0 / RWKV_HEAD_DIM) for n in every]
        for n, (sq, gi) in enumerate(chains):
            ec = ec_ref[sq, pl.ds(ci, 1), lanes[n]]
            s_ref[sq, gi] = st[n] * ec + jnp.where(bd_b, upd[n], 0.0)
            mean = mom[n][0:c]
            var = mom[n][c:2 * c] - mean * mean
            yn = (y[n] - mean) * lax.rsqrt(var + RWKV_LN_EPS) * lnw_ref[:, lanes[n]] + lnb_ref[:, lanes[n]]
            yn = (yn + at(bon_ref, n).astype(F32)) * at(g_ref, n).astype(F32)
            o_ref[sq, rows, lanes[n]] = yn.astype(o_ref.dtype)
        return carry

    lax.fori_loop(0, kap_ref.shape[1] // c, chunk, 0)


SEQ_PER_STEP = 4
REC_BLOCK = 256


def _seq_per_step(b):
    return max(n for n in range(1, SEQ_PER_STEP + 1) if b % n == 0)


def _per_chunk_spec(x, nsq):
    b, nc, w = x.shape
    cps = REC_BLOCK // CHUNK
    return x.reshape(b, nc // cps, cps, w), pl.BlockSpec((nsq, None, cps, w), lambda bi, i: (bi, i, 0, 0))


def _rwkv_chunk(ops, ec, ln_w, ln_b):
    b, s, w = ops[0].shape
    nsq = _seq_per_step(b)
    lblk = REC_BLOCK
    seq = pl.BlockSpec((nsq, lblk, w), lambda bi, i: (bi, i, 0))
    par = pl.BlockSpec((1, w), lambda bi, i: (0, 0))
    gw = HEADS_PER_PASS * RWKV_HEAD_DIM
    ec4, ec_spec = _per_chunk_spec(ec, nsq)
    return pl.pallas_call(
        _rwkv_chunk_kernel,
        grid=(b // nsq, s // lblk),
        in_specs=[seq] * 9 + [ec_spec, par, par],
        out_specs=seq,
        out_shape=jax.ShapeDtypeStruct((b, s, w), BF16),
        scratch_shapes=[pltpu.VMEM((nsq, w // gw, gw, gw), F32)],
        compiler_params=_cp("parallel", "arbitrary"),
        name="rwkv_chunk",
    )(*ops, ec4, ln_w.reshape(1, w), ln_b.reshape(1, w))


def _gdn_prep_kernel(x_ref, xp_ref, sm_ref, z_ref, cw_ref, alog_ref, dtb_ref, ltri_ref, utri_ref, same_ref, tot_ref,
                     q_ref, k_ref, kb_ref, vb_ref, kbe_ref, qg_ref, kg_ref, zs_ref, dec_ref, egl_ref):
    first = pl.program_id(1) == 0
    dh = GDN_HEAD_DIM
    w = MIX_W
    x = x_ref[...].astype(F32)
    ts = x.shape[0]
    prev = jnp.where(first, 0.0, xp_ref[...].astype(F32))
    row = lax.broadcasted_iota(jnp.int32, x.shape, 0)
    acc = x * cw_ref[GDN_CONV - 1:GDN_CONV, :]
    for d in range(1, GDN_CONV):
        sh = pltpu.roll(x, d, 0)
        for rr in range(d):
            sh = jnp.where(row == rr, prev[PREV_ROWS - d + rr:PREV_ROWS - d + rr + 1, :], sh)
        acc = acc + sh * cw_ref[GDN_CONV - 1 - d:GDN_CONV - d, :]
    act = acc * _sigmoid(acc)
    sm = sm_ref[...].astype(F32)
    gs = [-jnp.exp(alog_ref[:, h:h + 1]) * _softplus(sm[:, 16 + h:17 + h] + dtb_ref[:, h:h + 1])
          for h in range(GDN_HEADS)]
    gc = jnp.concatenate([jnp.broadcast_to(g, (ts, CHUNK)) for g in gs], axis=1)
    gc3 = _split3(gc)
    gamc = _sum01(ltri_ref[...], gc3)
    sufc = _sum01(utri_ref[...], gc3)
    t_in = lax.broadcasted_iota(jnp.int32, gc.shape, 0) % CHUNK
    s_in = lax.broadcasted_iota(jnp.int32, gc.shape, 1) % CHUNK
    gamr = _sum01(same_ref[...], jnp.where(t_in == s_in, gamc, 0.0))
    dec_ref[...] = jnp.exp(jnp.where(t_in >= s_in, gamc - gamr, NEG))
    eglc = jnp.exp(_sum01(tot_ref[...], gc3))[0:egl_ref.shape[0]]
    egl_ref[...] = jnp.concatenate([eglc[:, h * CHUNK:(h + 1) * CHUNK] for h in range(GDN_HEADS)
                                    for _ in range(dh // CHUNK)],
                                   axis=1)
    for h in range(GDN_HEADS):
        ls = slice(h * dh, (h + 1) * dh)
        q = act[:, h * dh:(h + 1) * dh]
        k = act[:, w + h * dh:w + (h + 1) * dh]
        v = act[:, 2 * w + h * dh:2 * w + (h + 1) * dh]
        q = q * lax.rsqrt(jnp.sum(q * q, axis=-1, keepdims=True) + EPS) * (dh ** -0.5)
        k = k * lax.rsqrt(jnp.sum(k * k, axis=-1, keepdims=True) + EPS)
        beta = _sigmoid(sm[:, 20 + h:21 + h])
        eg = jnp.exp(gamc[:, h * CHUNK:h * CHUNK + 1])
        es = jnp.exp(sufc[:, h * CHUNK:h * CHUNK + 1])
        kb = k * beta
        q_ref[:, ls] = q.astype(BF16)
        k_ref[:, ls] = k.astype(BF16)
        kb_ref[:, ls] = kb.astype(BF16)
        vb_ref[:, ls] = (v * beta).astype(BF16)
        kbe_ref[:, ls] = (kb * eg).astype(BF16)
        qg_ref[:, ls] = (q * eg).astype(BF16)
        kg_ref[:, ls] = (k * es).astype(BF16)
    z = z_ref[...].astype(F32)
    zs_ref[...] = (z * _sigmoid(z)).astype(BF16)


def _gdn_prep(proj3, conv_w, a_log, dt_bias):
    b, s, _ = proj3.shape
    w = MIX_W
    ts = PREP_TILE
    nrb = ts // PREV_ROWS
    nck = ts // CHUNK
    cw = HEADS_PER_PASS * CHUNK
    ltri, utri, same, tot = _chunk_consts(ts)
    full = lambda shp: pl.BlockSpec(shp, lambda bi, i: (0,) * len(shp))
    seq = pl.BlockSpec((None, ts, w), lambda bi, i: (bi, i, 0))
    return pl.pallas_call(
        _gdn_prep_kernel,
        grid=(b, s // ts),
        in_specs=[pl.BlockSpec((None, ts, 3 * w), lambda bi, i: (bi, i, C_GDN // (3 * w))),
                  pl.BlockSpec((None, PREV_ROWS, 3 * w),
                               lambda bi, i: (bi, jnp.maximum(i * nrb - 1, 0), C_GDN // (3 * w))),
                  pl.BlockSpec((None, ts, 128), lambda bi, i: (bi, i, C_SM0 // 128)),
                  pl.BlockSpec((None, ts, w), lambda bi, i: (bi, i, (C_GDN + 3 * w) // w)),
                  full((GDN_CONV, 3 * w)), full((1, GDN_HEADS)), full((1, GDN_HEADS)),
                  full((ts, ts)), full((ts, ts)), full((ts, ts)), full(tot.shape)],
        out_specs=[seq] * 8 + [pl.BlockSpec((None, ts, cw), lambda bi, i: (bi, i, 0)),
                               pl.BlockSpec((None, nck, w), lambda bi, i: (bi, i, 0))],
        out_shape=[jax.ShapeDtypeStruct((b, s, w), BF16)] * 8 + [jax.ShapeDtypeStruct((b, s, cw), F32),
                                                                 jax.ShapeDtypeStruct((b, s // CHUNK, w), F32)],
        compiler_params=_cp("parallel", "parallel"),
        name="gdn_prep",
    )(proj3, proj3, proj3, proj3, conv_w, a_log.reshape(1, GDN_HEADS), dt_bias.reshape(1, GDN_HEADS),
      ltri, utri, same, tot)


def _gdn_chunk_kernel(q_ref, k_ref, kb_ref, vb_ref, kbe_ref, qg_ref, kg_ref, zs_ref, dec_ref, egl_ref, nw_ref,
                      o_ref, s_ref):
    c = CHUNK
    dh = GDN_HEAD_DIM
    w = MIX_W
    pw = 2 * dh
    npair = w // pw
    nseq = q_ref.shape[0]

    @pl.when(pl.program_id(1) == 0)
    def _():
        s_ref[...] = jnp.zeros_like(s_ref)

    _, strict, eye = _cat_tri_masks()
    cw = HEADS_PER_PASS * c
    bd64 = jnp.where(_block_mask(cw, cw, c, c), 1.0, 0.0).astype(BF16)
    bdk = jnp.where(_block_mask(cw, w, c, dh), 1.0, 0.0).astype(BF16)
    bdp_b = _block_mask(pw, pw, dh, dh)

    seqs = range(nseq)
    pairs = [slice(p * pw, (p + 1) * pw) for p in range(npair)]

    def chunk(ci, carry):
        rows = pl.ds(pl.multiple_of(ci * c, c), c)
        lhs = [jnp.concatenate([kb_ref[sq, rows, :], q_ref[sq, rows, :]], axis=0) for sq in seqs]
        aq = [_dot_nt(lhs[sq], _bd_tile(k_ref[sq, rows, :], bdk)) for sq in seqs]
        dec = [dec_ref[sq, rows, :] for sq in seqs]
        p = [-jnp.where(strict, aq[sq][0:c] * dec[sq], 0.0) for sq in seqs]
        t = [jnp.where(eye, 1.0, 0.0) + p[sq] for sq in seqs]
        for _ in range(int(math.log2(c)) - 1):
            pb = [x.astype(BF16) for x in p]
            p = [_dot(pb[sq], _bd_tile(pb[sq], bd64)) for sq in seqs]
            t = [t[sq] + _dot(t[sq].astype(BF16), _bd_tile(p[sq].astype(BF16), bd64)) for sq in seqs]
        tb = [x.astype(BF16) for x in t]
        u = [_dot(tb[sq], _bd_tile(vb_ref[sq, rows, :], bdk)) for sq in seqs]
        wm = [_dot(tb[sq], _bd_tile(kbe_ref[sq, rows, :], bdk)).astype(BF16) for sq in seqs]
        st = [[s_ref[sq, pi] for pi in range(npair)] for sq in seqs]
        stb = [[x.astype(BF16) for x in st[sq]] for sq in seqs]
        ws = [jnp.concatenate([_dot(wm[sq][:, ps], stb[sq][pi]) for pi, ps in enumerate(pairs)], axis=1)
              for sq in seqs]
        vnb = [(u[sq] - ws[sq]).astype(BF16) for sq in seqs]
        qs = [jnp.concatenate([_dot(qg_ref[sq, rows, ps], stb[sq][pi]) for pi, ps in enumerate(pairs)], axis=1)
              for sq in seqs]
        o = [qs[sq] + _dot((aq[sq][c:2 * c] * dec[sq]).astype(BF16), _bd_tile(vnb[sq], bdk)) for sq in seqs]
        upd = [[_dot_tn(kg_ref[sq, rows, ps], vnb[sq][:, ps]) for ps in pairs] for sq in seqs]
        for sq in seqs:
            egl = egl_ref[sq, pl.ds(ci, 1), :]
            for pi, ps in enumerate(pairs):
                s_ref[sq, pi] = st[sq][pi] * egl[:, ps] + jnp.where(bdp_b, upd[sq][pi], 0.0)
            for h in range(GDN_HEADS):
                ls = slice(h * dh, (h + 1) * dh)
                oh = o[sq][:, ls]
                ms = jnp.mean(oh * oh, axis=-1, keepdims=True)
                o_ref[sq, rows, ls] = (oh * lax.rsqrt(ms + EPS) * nw_ref[...]
                                       * zs_ref[sq, rows, ls].astype(F32)).astype(o_ref.dtype)
        return carry

    lax.fori_loop(0, q_ref.shape[1] // c, chunk, 0)


def _gdn_chunk(ops, dec, egl, norm_w):
    b, s, w = ops[0].shape
    dh = GDN_HEAD_DIM
    nsq = _seq_per_step(b)
    lblk = REC_BLOCK
    seq = pl.BlockSpec((nsq, lblk, w), lambda bi, i: (bi, i, 0))
    egl, egl_spec = _per_chunk_spec(egl, nsq)
    return pl.pallas_call(
        _gdn_chunk_kernel,
        grid=(b // nsq, s // lblk),
        in_specs=[seq] * 8 + [pl.BlockSpec((nsq, lblk, dec.shape[2]), lambda bi, i: (bi, i, 0)), egl_spec,
                              pl.BlockSpec((1, dh), lambda bi, i: (0, 0))],
        out_specs=seq,
        out_shape=jax.ShapeDtypeStruct((b, s, w), BF16),
        scratch_shapes=[pltpu.VMEM((nsq, w // (2 * dh), 2 * dh, 2 * dh), F32)],
        compiler_params=_cp("parallel", "arbitrary"),
        name="gdn_chunk",
    )(*ops, dec, egl, norm_w.reshape(1, dh))


def _merge_kernel(x_ref, oa_ref, ob_ref, oc_ref, ga_ref, gb_ref, gc_ref, wb_ref, wo_ref, o_ref):
    merged = None
    for j, (br, gate) in enumerate(((oa_ref, ga_ref), (ob_ref, gb_ref), (oc_ref, gc_ref))):
        t = _sigmoid(gate[...].astype(F32)) * _dot(br[...], wb_ref[j])
        merged = t if merged is None else merged + t
    o_ref[...] = x_ref[...] + _dot(merged.astype(BF16), wo_ref[...])


def _merge(x2, o_a, o_b, o_c, proj, w_branch, w_out, tm=512):
    t, d = x2.shape
    w = MIX_W
    tm = min(tm, t)
    br = pl.BlockSpec((tm, w), lambda i: (i, 0))
    gate = lambda j: pl.BlockSpec((tm, d), lambda i: (i, C_GATE // d + j))
    return pl.pallas_call(
        _merge_kernel,
        grid=(t // tm,),
        in_specs=[pl.BlockSpec((tm, d), lambda i: (i, 0)), br, br, br, gate(0), gate(1), gate(2),
                  pl.BlockSpec((3, w, d), lambda i: (0, 0, 0)),
                  pl.BlockSpec((d, d), lambda i: (0, 0))],
        out_specs=pl.BlockSpec((tm, d), lambda i: (i, 0)),
        out_shape=jax.ShapeDtypeStruct((t, d), F32),
        compiler_params=_cp("parallel"),
        name="merge",
    )(x2, o_a.reshape(t, w), o_b.reshape(t, w), o_c.reshape(t, w), proj, proj, proj,
      w_branch.astype(BF16), w_out.astype(BF16))


def _ffn_kernel(x_ref, g_ref, w1_ref, w2_ref, o_ref, hn_ref):
    j = pl.program_id(1)

    @pl.when(j == 0)
    def _():
        x = x_ref[...]
        ms = jnp.mean(x * x, axis=-1, keepdims=True)
        hn_ref[...] = (x * lax.rsqrt(ms + EPS) * g_ref[...]).astype(BF16)
        o_ref[...] = x

    h1 = jnp.maximum(_dot(hn_ref[...], w1_ref[...]), 0.0)
    o_ref[...] += _dot((h1 * h1).astype(BF16), w2_ref[...])


def _ffn(x2, g, w1, w2, tm=1024, tf=512):
    t, d = x2.shape
    f = w1.shape[1]
    tm = min(tm, t)
    return pl.pallas_call(
        _ffn_kernel,
        grid=(t // tm, f // tf),
        in_specs=[pl.BlockSpec((tm, d), lambda i, j: (i, 0)),
                  pl.BlockSpec((1, d), lambda i, j: (0, 0)),
                  pl.BlockSpec((d, tf), lambda i, j: (0, j)),
                  pl.BlockSpec((tf, d), lambda i, j: (j, 0))],
        out_specs=pl.BlockSpec((tm, d), lambda i, j: (i, 0)),
        out_shape=jax.ShapeDtypeStruct((t, d), F32),
        scratch_shapes=[pltpu.VMEM((tm, d), BF16)],
        compiler_params=_cp("parallel", "arbitrary"),
        name="ffn",
    )(x2, g.reshape(1, d), w1.astype(BF16), w2.astype(BF16))


def kernel(x, norm_mix_g, w_in, nsa_q_norm, nsa_k_norm, nsa_cmp_pos, nsa_cmp_w1, nsa_cmp_w2, rwkv_mu, rwkv_w0, rwkv_w_up, rwkv_a0, rwkv_a_up, rwkv_g_up, rwkv_k_k, rwkv_k_a, rwkv_r_k, rwkv_ln_w, rwkv_ln_b, rwkv_v0, rwkv_vres_down, rwkv_vres_up, gdn_conv_w, gdn_a_log, gdn_dt_bias, gdn_norm_w, w_branch, w_out, norm_ffn_g, w_ff1, w_ff2):
    b, s, d = x.shape
    depth = w_in.shape[0]
    perm = jnp.asarray(_proj_perm())
    x2 = x.reshape(b * s, d)
    v_first = None
    for i in range(depth):
        vd = rwkv_vres_down[i - 1] if i > 0 else jnp.zeros((d, 32), F32)
        w_ext = jnp.concatenate([w_in[i], vd, jnp.zeros((d, 1), F32)], axis=1)
        w_all = jnp.take(w_ext, perm, axis=1).astype(BF16)
        proj = _rms_matmul(x2, norm_mix_g[i], w_all)
        proj3 = proj.reshape(b, s, N_PROJ)

        o_a = _nsa(proj3, nsa_q_norm[i], nsa_k_norm[i], nsa_cmp_pos[i], nsa_cmp_w1[i], nsa_cmp_w2[i])

        vres = None if i == 0 else (rwkv_v0[i - 1], rwkv_vres_up[i - 1])
        rw = _rwkv_prep(proj3, rwkv_mu[i], rwkv_w0[i], rwkv_w_up[i], rwkv_a0[i], rwkv_a_up[i], rwkv_g_up[i],
                        rwkv_k_k[i], rwkv_k_a[i], rwkv_r_k[i], v_first, vres)
        if i == 0:
            v_first = rw[10]
        o_b = _rwkv_chunk(rw[:9], rw[9], rwkv_ln_w[i], rwkv_ln_b[i])

        gd = _gdn_prep(proj3, gdn_conv_w[i], gdn_a_log[i], gdn_dt_bias[i])
        o_c = _gdn_chunk(gd[:8], gd[8], gd[9], gdn_norm_w[i])

        x2 = _merge(x2, o_a, o_b, o_c, proj, w_branch[i], w_out[i])
        x2 = _ffn(x2, norm_ffn_g[i], w_ff1[i], w_ff2[i])
    return x2.reshape(b, s, d)
```

```python
import functools
import math

import numpy as np
import jax
import jax.numpy as jnp
from jax import lax
from jax.experimental import pallas as pl
from jax.experimental.pallas import tpu as pltpu

F32 = jnp.float32
BF16 = jnp.bfloat16

D_MODEL = 1024
MIX_W = 512
NSA_HEAD_DIM = 64
NSA_KV_GROUPS = 2
NSA_HPG = 4
NSA_CMP_BLOCK = 32
NSA_CMP_STRIDE = 16
NSA_CMP_HIDDEN = 256
NSA_SEL_BLOCK = 64
NSA_N_SEL = 16
NSA_WINDOW = 512
Q_BLOCK = 128
NSA_IN = 1304
RWKV_HEADS = 8
RWKV_HEAD_DIM = 64
RWKV_LN_EPS = 64e-5
RWKV_IN = 1792
GDN_HEADS = 4
GDN_HEAD_DIM = 128
GDN_CONV = 4
GDN_IN = 2056
D_IN = 8224
D_FF = 4096
EPS = 1e-6
NEG = -1e30
FORCE = 1e4

C_KV = 0
C_Q = 768
C_SM0 = 1280
C_SM1 = 1408
C_RKV = 1536
C_GDN = 3072
C_GATE = 5120
C_LR = 8192
N_PROJ = 8448
VMEM_LIMIT = 48 * 1024 * 1024


def _proj_perm():
    zero = D_IN + 32
    perm = np.full((N_PROJ,), zero, np.int32)
    perm[C_KV:C_KV + 768] = np.arange(512, 1280)
    perm[C_Q:C_Q + 512] = np.arange(0, 512)
    perm[C_SM0:C_SM0 + 12] = 1280 + np.arange(12)
    perm[C_SM1:C_SM1 + 12] = 1292 + np.arange(12)
    rw = NSA_IN
    gd = NSA_IN + RWKV_IN
    gt = gd + GDN_IN
    perm[C_SM0 + 16:C_SM0 + 24] = gd + 2048 + np.arange(8)
    perm[C_SM0 + 32:C_SM0 + 64] = D_IN + np.arange(32)
    perm[C_RKV:C_RKV + 1536] = rw + np.arange(1536)
    perm[C_LR:C_LR + 256] = rw + 1536 + np.arange(256)
    perm[C_GDN:C_GDN + 2048] = gd + np.arange(2048)
    perm[C_GATE:C_GATE + 3072] = gt + np.arange(3072)
    return perm


def _cp(*sem):
    return pltpu.CompilerParams(dimension_semantics=sem, vmem_limit_bytes=VMEM_LIMIT)


def _dot(a, b):
    return jnp.dot(a, b, preferred_element_type=F32)


def _dot_nt(a, b):
    return lax.dot_general(a, b, (((1,), (1,)), ((), ())), preferred_element_type=F32)


def _dot_tn(a, b):
    return lax.dot_general(a, b, (((0,), (0,)), ((), ())), preferred_element_type=F32)


def _split(a):
    hi = a.astype(BF16)
    lo = (a - hi.astype(F32)).astype(BF16)
    return hi, lo


def _mm3(a, b):
    ah, al = _split(a)
    bh, bl = _split(b)
    return _dot(ah, bh) + (_dot(ah, bl) + _dot(al, bh))


def _split3(x):
    hi = x.astype(BF16)
    r1 = x - hi.astype(F32)
    mid = r1.astype(BF16)
    return hi, mid, (r1 - mid.astype(F32)).astype(BF16)


def _sum01(m, x):
    hi, mid, lo = x if isinstance(x, tuple) else _split3(x)
    return _dot(m, hi) + (_dot(m, mid) + _dot(m, lo))


def _segsum_left(m, x):
    hi, lo = _split(x)
    return _dot(m, hi) + _dot(m, lo)


def _segsum(x, seg):
    hi, lo = _split(x)
    return _dot(hi, seg) + _dot(lo, seg)


CHUNK = 64
HEADS_PER_PASS = 4


def _chunk_consts(ts):
    r = np.arange(ts)
    same = (r[:, None] // CHUNK) == (r[None, :] // CHUNK)
    ltri = same & (r[:, None] >= r[None, :])
    utri = same & (r[:, None] < r[None, :])
    nck = ts // CHUNK
    tot = np.zeros((max(nck, 8), ts), bool)
    tot[:nck] = (r[None, :] // CHUNK) == np.arange(nck)[:, None]
    return [jnp.asarray(m, BF16) for m in (ltri, utri, same, tot)]


def _block_mask(rows, cols, rblk, cblk):
    r = lax.broadcasted_iota(jnp.int32, (rows, cols), 0) // rblk
    c = lax.broadcasted_iota(jnp.int32, (rows, cols), 1) // cblk
    return r == c


def _cat_tri_masks():
    w = HEADS_PER_PASS * CHUNK
    t = lax.broadcasted_iota(jnp.int32, (CHUNK, w), 0)
    s = lax.broadcasted_iota(jnp.int32, (CHUNK, w), 1) % CHUNK
    return t >= s, t > s, t == s


def _bd_tile(y, bd):
    return jnp.concatenate([y] * HEADS_PER_PASS, axis=0) * bd


def _softplus(z):
    return jnp.maximum(z, 0.0) + jnp.log1p(jnp.exp(-jnp.abs(z)))


def _sigmoid(z):
    return 1.0 / (1.0 + jnp.exp(-z))


def _rms_matmul_kernel(x_ref, g_ref, w_ref, o_ref, hn_ref):
    @pl.when(pl.program_id(1) == 0)
    def _():
        x = x_ref[...]
        ms = jnp.mean(x * x, axis=-1, keepdims=True)
        hn_ref[...] = (x * lax.rsqrt(ms + EPS) * g_ref[...]).astype(BF16)

    o_ref[...] = _dot(hn_ref[...], w_ref[...]).astype(o_ref.dtype)


def _rms_matmul(x2, g, w, tm=1024, tn=768):
    t, d = x2.shape
    n = w.shape[1]
    tm = min(tm, t)
    return pl.pallas_call(
        _rms_matmul_kernel,
        grid=(t // tm, n // tn),
        in_specs=[pl.BlockSpec((tm, d), lambda i, j: (i, 0)),
                  pl.BlockSpec((1, d), lambda i, j: (0, 0)),
                  pl.BlockSpec((d, tn), lambda i, j: (0, j))],
        out_specs=pl.BlockSpec((tm, tn), lambda i, j: (i, j)),
        out_shape=jax.ShapeDtypeStruct((t, n), BF16),
        scratch_shapes=[pltpu.VMEM((tm, d), BF16)],
        compiler_params=_cp("parallel", "arbitrary"),
        name="rms_proj",
    )(x2, g.reshape(1, d), w)


def _nsa_kvprep_kernel(kv_ref, kg_ref, kc_ref, vc_ref, ks_ref, vs_ref, kw_ref, vw_ref):
    outs = (kc_ref, vc_ref, ks_ref, vs_ref, kw_ref, vw_ref)
    for j in range(6):
        for g in range(NSA_KV_GROUPS):
            lo = j * 128 + g * NSA_HEAD_DIM
            piece = kv_ref[:, lo:lo + NSA_HEAD_DIM].astype(F32)
            if j in (2, 4):
                gain = kg_ref[j // 2:j // 2 + 1, :]
                ms = jnp.mean(piece * piece, axis=-1, keepdims=True)
                piece = piece * lax.rsqrt(ms + EPS) * gain
            outs[j][g] = piece.astype(outs[j].dtype)


def _nsa_kvprep(proj3, k_g, ts=1024):
    b, s, _ = proj3.shape
    g, dh = NSA_KV_GROUPS, NSA_HEAD_DIM
    out_spec = lambda wd: pl.BlockSpec((None, g, ts, wd), lambda bi, i: (bi, 0, i, 0))
    shp = lambda dt, wd=dh: jax.ShapeDtypeStruct((b, g, s, wd), dt)
    return pl.pallas_call(
        _nsa_kvprep_kernel,
        grid=(b, s // ts),
        in_specs=[pl.BlockSpec((None, ts, 768), lambda bi, i: (bi, i, C_KV // 768)),
                  pl.BlockSpec((3, dh), lambda bi, i: (0, 0))],
        out_specs=[out_spec(dh)] * 6,
        out_shape=[shp(F32), shp(F32), shp(BF16), shp(BF16), shp(BF16), shp(BF16)],
        compiler_params=_cp("parallel", "parallel"),
        name="nsa_kvprep",
    )(proj3, k_g)


def _nsa_compress_kernel(x_ref, w1_ref, pos_ref, w2_ref, g_ref, o_ref, *, normalize):
    x = x_ref[...].astype(BF16)
    half = x.shape[1]
    a = _dot(x, w1_ref[0:half, :])
    bm = _dot(x, w1_ref[half:2 * half, :])
    bias = _dot(pos_ref[...].astype(BF16), w1_ref[...])[0:1, :]
    n = x.shape[0]
    hid = a + pltpu.roll(bm, n - 1, 0) + bias
    hid = hid * _sigmoid(hid)
    out = _dot(hid.astype(BF16), w2_ref[...])
    if normalize:
        ms = jnp.mean(out * out, axis=-1, keepdims=True)
        out = out * lax.rsqrt(ms + EPS) * g_ref[...]
    o_ref[...] = out.astype(o_ref.dtype)


def _nsa_compress(xh, w1, pos, w2, gain, normalize):
    bg, n, half = xh.shape
    dh = NSA_HEAD_DIM
    pos8 = jnp.broadcast_to(pos.reshape(1, 2 * half), (8, 2 * half))
    return pl.pallas_call(
        functools.partial(_nsa_compress_kernel, normalize=normalize),
        grid=(bg,),
        in_specs=[pl.BlockSpec((None, n, half), lambda i: (i, 0, 0)),
                  pl.BlockSpec((2 * half, NSA_CMP_HIDDEN), lambda i: (0, 0)),
                  pl.BlockSpec((8, 2 * half), lambda i: (0, 0)),
                  pl.BlockSpec((NSA_CMP_HIDDEN, dh), lambda i: (0, 0)),
                  pl.BlockSpec((1, dh), lambda i: (0, 0))],
        out_specs=pl.BlockSpec((None, n, dh), lambda i: (i, 0, 0)),
        out_shape=jax.ShapeDtypeStruct((bg, n, dh), BF16),
        compiler_params=_cp("parallel"),
        name="nsa_compress",
    )(xh, w1.astype(BF16), pos8, w2.astype(BF16), gain.reshape(1, dh))


SEL_TILE = 256
STRIP = 128
LOG2E = 1.4426950408889634


def _softmax_cols(s, valid):
    s = jnp.where(valid, s, NEG)
    p = jnp.exp2(s - jnp.max(s, axis=0, keepdims=True))
    return p, jnp.sum(p, axis=0, keepdims=True)


def _nsa_attn_t_kernel(q_ref, sm_ref, kc_ref, vct_ref, ks_ref, vst_ref, kw_ref, vwt_ref, c2s_ref, qg_ref, o_ref,
                       sb_ref, acc_ref, al_ref, m_ref, l_ref, sa_ref, sb2_ref, pa_ref, pb_ref, *, n_sel):
    dh, hg, qb = NSA_HEAD_DIM, NSA_HPG, Q_BLOCK
    cols = hg * qb
    tk = SEL_TILE
    blk = pl.program_id(2)
    start = blk * qb

    xt = jnp.transpose(q_ref[...].astype(F32))
    qs = []
    for h in range(hg):
        xh = xt[h * dh:(h + 1) * dh, :]
        ms = jnp.mean(xh * xh, axis=0, keepdims=True)
        qs.append(xh * lax.rsqrt(ms + EPS) * qg_ref[...])
    qt = jnp.concatenate(qs, axis=1).astype(BF16)
    tq_l =start + lax.broadcasted_iota(jnp.int32, (1, qb), 1)
    heads = [slice(h * qb, (h + 1) * qb) for h in range(hg)]

    t_col = jnp.concatenate([tq_l] * hg, axis=1)

    n_cmp = kc_ref.shape[0]
    cvalid = (lax.broadcasted_iota(jnp.int32, (n_cmp, 1), 0) * NSA_CMP_STRIDE + (NSA_CMP_BLOCK - 1)) <= t_col
    p, l = _softmax_cols(_dot(kc_ref[...], qt), cvalid)
    p = p * jnp.where(t_col >= NSA_CMP_BLOCK - 1, 1.0 / l, 0.0)
    o_c = _dot(vct_ref[...], p.astype(BF16))
    psum = p[:, heads[0]]
    for cs in heads[1:]:
        psum = psum + p[:, cs]
    imp_t = _segsum_left(c2s_ref[...], psum)

    n_slc = imp_t.shape[0]
    jr = lax.broadcasted_iota(jnp.int32, (n_slc, 1), 0)
    cur = jnp.right_shift(tq_l, int(math.log2(NSA_SEL_BLOCK)))
    forced = (jr == 0) | (jr == cur) | (jr == cur - 1)
    causal = jr * NSA_SEL_BLOCK <= tq_l
    val = jnp.where(forced, -jnp.inf, jnp.where(causal, imp_t, -FORCE))
    jrf = jr.astype(F32)

    wlen = NSA_WINDOW + qb
    base = pl.multiple_of(jnp.maximum(start - NSA_WINDOW, 0), qb)
    dist = t_col - (base + lax.broadcasted_iota(jnp.int32, (wlen, 1), 0))
    in_window = lax.bitcast_convert_type(dist, jnp.uint32) < jnp.uint32(NSA_WINDOW)
    p, l = _softmax_cols(_dot(kw_ref[pl.ds(base, wlen), :], qt), in_window)
    o_w = _dot(vwt_ref[:, pl.ds(base, wlen)], p.astype(BF16)) * (1.0 / l)

    for _ in range(n_sel - 3):
        mx = jnp.max(val, axis=0, keepdims=True)
        idx = jnp.min(jnp.where(val == mx, jrf, float(n_slc)), axis=0, keepdims=True)
        val = jnp.where(jrf == idx, -jnp.inf, val)
    sb_ref[...] = jnp.where(val == -jnp.inf, 0.0, NEG)

    bpt = tk // NSA_SEL_BLOCK
    last = start // tk
    n_pairs = (last + 1) // 2
    strips = [(slice(c, c + STRIP), c % qb) for c in range(0, cols, STRIP)]

    def key_off(i):
        return pl.multiple_of(jnp.minimum(i, last) * tk, tk)

    def scores(i):
        return _dot(ks_ref[pl.ds(key_off(i), tk), :], qt)

    def step(tile, prev_tile, next_tile, s_cur, s_nxt, p_cur, p_prev, first):
        s_nxt[...] = scores(next_tile)
        if not first:
            pv = _dot(vst_ref[:, pl.ds(key_off(prev_tile), tk)], p_prev[...])
        rows = [sb_ref[pl.ds(jnp.minimum(tile, last) * bpt + j, 1), :] for j in range(bpt)]
        if first:
            kpos = tile * tk + lax.broadcasted_iota(jnp.int32, (tk, 1), 0)
        else:
            rows = [jnp.where(tile < last, r, NEG) for r in rows]
        for cs, off in strips:
            ts_ = slice(off, off + STRIP)
            sh = jnp.concatenate([s_cur[j * NSA_SEL_BLOCK:(j + 1) * NSA_SEL_BLOCK, cs] + rows[j][:, ts_]
                                  for j in range(bpt)], axis=0)
            if first:
                sh = jnp.where(kpos <= tq_l[:, ts_], sh, NEG)
            m_old = m_ref[:, cs]
            m_new = jnp.maximum(m_old, jnp.max(sh, axis=0, keepdims=True))
            alpha = jnp.exp2(m_old - m_new)
            p = jnp.exp2(sh - m_new)
            p_cur[:, cs] = p.astype(BF16)
            m_ref[:, cs] = m_new
            l_ref[:, cs] = alpha * l_ref[:, cs] + jnp.sum(p, axis=0, keepdims=True)
            if not first:
                acc_ref[:, cs] = al_ref[:, cs] * acc_ref[:, cs] + pv[:, cs]
            al_ref[:, cs] = alpha

    m_ref[...] = jnp.full((1, cols), NEG, F32)
    l_ref[...] = jnp.zeros((1, cols), F32)
    acc_ref[...] = jnp.zeros((dh, cols), F32)
    sa_ref[...] = scores(last)
    step(last, last, 0, sa_ref, sb2_ref, pa_ref, pb_ref, True)

    def sel_body(j, carry):
        t0 = 2 * j
        step(t0, jnp.where(j == 0, last, t0 - 1), t0 + 1, sb2_ref, sa_ref, pb_ref, pa_ref, False)
        step(t0 + 1, t0, t0 + 2, sa_ref, sb2_ref, pa_ref, pb_ref, False)
        return carry

    lax.fori_loop(0, n_pairs, sel_body, 0)
    prev = jnp.where(n_pairs == 0, last, 2 * n_pairs - 1)
    acc = al_ref[...] * acc_ref[...] + _dot(vst_ref[:, pl.ds(key_off(prev), tk)], pa_ref[...])
    o_s = acc * (1.0 / l_ref[...])

    gt = jnp.transpose(_sigmoid(sm_ref[...].astype(F32)))
    outs = []
    for h, cs in enumerate(heads):
        outs.append(gt[3 * h:3 * h + 1, :] * o_c[:, cs] + gt[3 * h + 1:3 * h + 2, :] * o_s[:, cs]
                    + gt[3 * h + 2:3 * h + 3, :] * o_w[:, cs])
    o_ref[...] = jnp.transpose(jnp.concatenate(outs, axis=0)).astype(o_ref.dtype)


def _nsa(proj3, q_g, k_g, cmp_pos, cmp_w1, cmp_w2):
    b, s, _ = proj3.shape
    g, dh, qb = NSA_KV_GROUPS, NSA_HEAD_DIM, Q_BLOCK
    kc_raw, vc_raw, ks, vs, kw, vw = _nsa_kvprep(proj3, k_g, ts=min(1024, s))
    n_half = s // NSA_CMP_STRIDE
    half = NSA_CMP_STRIDE * dh
    kc = _nsa_compress(kc_raw.reshape(b * g, n_half, half), cmp_w1[0], cmp_pos[0], cmp_w2[0], k_g[0], True)
    vc = _nsa_compress(vc_raw.reshape(b * g, n_half, half), cmp_w1[1], cmp_pos[1], cmp_w2[1], k_g[0], False)
    kc = kc.reshape(b, g, n_half, dh)
    vc = vc.reshape(b, g, n_half, dh)

    n_slc = s // NSA_SEL_BLOCK
    n_sel = min(NSA_N_SEL, n_slc)
    cmp_start = np.arange(n_half) * NSA_CMP_STRIDE
    slc_start = np.arange(n_slc) * NSA_SEL_BLOCK
    overlap = np.clip(np.minimum(cmp_start[:, None] + NSA_CMP_BLOCK, slc_start[None, :] + NSA_SEL_BLOCK)
                      - np.maximum(cmp_start[:, None], slc_start[None, :]), 0, None)
    c2s = (overlap / NSA_CMP_BLOCK).T

    k_spec = lambda n, wd=dh: pl.BlockSpec((None, None, n, wd), lambda bi, gi, i: (bi, gi, 0, 0))
    vt_spec = lambda n: pl.BlockSpec((None, None, dh, n), lambda bi, gi, i: (bi, gi, 0, 0))
    tr = lambda v: jnp.swapaxes(v, 2, 3)
    cols = NSA_HPG * qb
    q_gain = jnp.broadcast_to((q_g * (dh ** -0.5 * LOG2E)).reshape(dh, 1), (dh, qb))
    return pl.pallas_call(
        functools.partial(_nsa_attn_t_kernel, n_sel=n_sel),
        grid=(b, g, s // qb),
        in_specs=[pl.BlockSpec((None, qb, 256), lambda bi, gi, i: (bi, i, C_Q // 256 + gi)),
                  pl.BlockSpec((None, qb, 128), lambda bi, gi, i: (bi, i, C_SM0 // 128 + gi)),
                  k_spec(n_half), vt_spec(n_half), k_spec(s), vt_spec(s), k_spec(s), vt_spec(s),
                  pl.BlockSpec((n_slc, n_half), lambda bi, gi, i: (0, 0)),
                  pl.BlockSpec((dh, qb), lambda bi, gi, i: (0, 0))],
        out_specs=pl.BlockSpec((None, qb, 256), lambda bi, gi, i: (bi, i, gi)),
        out_shape=jax.ShapeDtypeStruct((b, s, MIX_W), BF16),
        scratch_shapes=([pltpu.VMEM((n_slc, qb), F32), pltpu.VMEM((dh, cols), F32),
                         pltpu.VMEM((1, cols), F32), pltpu.VMEM((1, cols), F32), pltpu.VMEM((1, cols), F32)]
                        + [pltpu.VMEM((SEL_TILE, cols), F32)] * 2 + [pltpu.VMEM((SEL_TILE, cols), BF16)] * 2),
        compiler_params=_cp("parallel", "parallel", "arbitrary"),
        name="nsa_attn",
    )(proj3, proj3, kc, tr(vc), ks, tr(vs), kw, tr(vw), jnp.asarray(c2s, BF16), q_gain)


def _prev_rows(prev_ref, first):
    return jnp.where(first, 0.0, prev_ref[PREV_ROWS - 8:PREV_ROWS, :].astype(F32))


def _shift_down(x, prev8, d):
    rolled = pltpu.roll(x, d, 0)
    row8 = lax.broadcasted_iota(jnp.int32, prev8.shape, 0)
    top = jnp.where(row8 < d, pltpu.roll(prev8, d, 0), rolled[0:8])
    return jnp.concatenate([top, rolled[8:]], axis=0)


PREP_TILE = 512
PREV_ROWS = 16


def _rwkv_prep_kernel(*refs, has_vres):
    (rkv_ref, rkvp_ref, lr_ref, lrp_ref, mu1_ref, mu2_ref, w0_ref, wup_ref, a0_ref, aup_ref, gup_ref,
     kkp_ref, kap_ref, rkp_ref, seg_ref, ltri_ref, utri_ref, tot_ref) = refs[:18]
    if has_vres:
        sm_ref, vf_ref, v0_ref, vu_ref = refs[18:22]
        outs = refs[22:]
    else:
        outs = refs[18:]
    kapo_ref, beto_ref, kto_ref, rto_ref, ktc_ref, betc_ref, vo_ref, bon_ref, g_ref, ec_ref = outs[:10]
    first = pl.program_id(1) == 0
    w = MIX_W
    c = rkv_ref[...].astype(F32)
    c = c + (_shift_down(c, _prev_rows(rkvp_ref, first), 1) - c) * mu1_ref[...]
    lr = lr_ref[...].astype(F32)
    lr = lr + (_shift_down(lr, _prev_rows(lrp_ref, first), 1) - lr) * mu2_ref[...]
    r, k, v = c[:, 0:w], c[:, w:2 * w], c[:, 2 * w:3 * w]
    wd, ad, gd = lr[:, 0:64], lr[:, 64:128], lr[:, 128:256]
    wlog = -_softplus(-(w0_ref[...] + _mm3(jnp.tanh(wd), wup_ref[...]))) - 0.5
    lw = -jnp.exp(wlog)
    a = _sigmoid(a0_ref[...] + _mm3(ad, aup_ref[...]))
    g_ref[...] = _mm3(_sigmoid(gd), gup_ref[...]).astype(g_ref.dtype)
    if has_vres:
        mix = _sigmoid(v0_ref[...] + _mm3(sm_ref[:, 32:64].astype(F32), vu_ref[...]))
        v = v + (vf_ref[...] - v) * mix
    else:
        outs[10][...] = v
    seg = seg_ref[...]
    kk = k * kkp_ref[...]
    kk = kk * lax.rsqrt(_segsum(kk * kk, seg) + EPS)
    k = k * (1.0 + (a - 1.0) * kap_ref[...])
    kka = kk * a
    lw3 = _split3(lw)
    lcum = _sum01(ltri_ref[...], lw3)
    e_suf = jnp.exp(_sum01(utri_ref[...], lw3))
    e_inv = jnp.exp(-lcum)
    kapo_ref[...] = (kk * jnp.exp(lcum - lw)).astype(BF16)
    beto_ref[...] = (kka * e_inv).astype(BF16)
    kto_ref[...] = (k * e_inv).astype(BF16)
    rto_ref[...] = (r * jnp.exp(lcum)).astype(BF16)
    ktc_ref[...] = (k * e_suf).astype(BF16)
    betc_ref[...] = (kka * e_suf).astype(BF16)
    vo_ref[...] = v.astype(BF16)
    bon_ref[...] = (_segsum(r * k * rkp_ref[...], seg) * v).astype(BF16)
    ec_ref[...] = jnp.exp(_sum01(tot_ref[...], lw3))[0:ec_ref.shape[0]]


def _rwkv_prep(proj3, mu, w0, w_up, a0, a_up, g_up, k_k, k_a, r_k, v_first, vres):
    b, s, _ = proj3.shape
    w = MIX_W
    has_vres = vres is not None
    ts = PREP_TILE
    nrb = ts // PREV_ROWS
    nck = ts // CHUNK
    cur = lambda wd, cb: pl.BlockSpec((None, ts, wd), lambda bi, i: (bi, i, cb))
    prev = lambda wd, cb: pl.BlockSpec((None, PREV_ROWS, wd), lambda bi, i: (bi, jnp.maximum(i * nrb - 1, 0), cb))
    full = lambda shp: pl.BlockSpec(shp, lambda bi, i: (0,) * len(shp))
    ltri, utri, _, tot = _chunk_consts(ts)
    hd = np.arange(w) // RWKV_HEAD_DIM
    seg = jnp.asarray(hd[:, None] == hd[None, :], BF16)
    in_specs = [cur(1536, C_RKV // 1536), prev(1536, C_RKV // 1536), cur(256, C_LR // 256), prev(256, C_LR // 256),
                full((1, 1536)), full((1, 256)), full((1, w)), full((64, w)), full((1, w)), full((64, w)),
                full((128, w)), full((1, w)), full((1, w)), full((1, w)), full((w, w)), full((ts, ts)),
                full((ts, ts)), full(tot.shape)]
    args = [proj3, proj3, proj3, proj3, mu[:1536].reshape(1, 1536), mu[1536:].reshape(1, 256), w0.reshape(1, w),
            w_up, a0.reshape(1, w), a_up, g_up, k_k.reshape(1, w), k_a.reshape(1, w), r_k.reshape(1, w),
            seg, ltri, utri, tot]
    if has_vres:
        v0, vu = vres
        in_specs += [cur(128, C_SM0 // 128), pl.BlockSpec((None, ts, w), lambda bi, i: (bi, i, 0)),
                     full((1, w)), full((32, w))]
        args += [proj3, v_first, v0.reshape(1, w), vu]
    seq = pl.BlockSpec((None, ts, w), lambda bi, i: (bi, i, 0))
    out_specs = [seq] * 9 + [pl.BlockSpec((None, nck, w), lambda bi, i: (bi, i, 0))]
    out_shape = [jax.ShapeDtypeStruct((b, s, w), BF16)] * 9 + [jax.ShapeDtypeStruct((b, s // CHUNK, w), F32)]
    if not has_vres:
        out_specs.append(seq)
        out_shape.append(jax.ShapeDtypeStruct((b, s, w), F32))
    return pl.pallas_call(
        functools.partial(_rwkv_prep_kernel, has_vres=has_vres),
        grid=(b, s // ts),
        in_specs=in_specs,
        out_specs=out_specs,
        out_shape=out_shape,
        compiler_params=_cp("parallel", "parallel"),
        name="rwkv_prep",
    )(*args)


def _rwkv_chunk_kernel(kap_ref, bet_ref, kt_ref, rt_ref, ktc_ref, betc_ref, v_ref, bon_ref, g_ref, ec_ref,
                       lnw_ref, lnb_ref, o_ref, s_ref):
    c = CHUNK
    gw = HEADS_PER_PASS * RWKV_HEAD_DIM
    nseq = kap_ref.shape[0]
    ngrp = kap_ref.shape[2] // gw

    @pl.when(pl.program_id(1) == 0)
    def _():
        s_ref[...] = jnp.zeros_like(s_ref)

    incl, strict, eye = _cat_tri_masks()
    bd_b = _block_mask(gw, gw, RWKV_HEAD_DIM, RWKV_HEAD_DIM)
    bd = jnp.where(bd_b, 1.0, 0.0).astype(BF16)

    chains = [(sq, gi) for sq in range(nseq) for gi in range(ngrp)]
    lanes = [slice(gi * gw, (gi + 1) * gw) for _, gi in chains]
    every = range(len(chains))

    def chunk(ci, carry):
        rows = pl.ds(pl.multiple_of(ci * c, c), c)
        at = lambda ref, n: ref[chains[n][0], rows, lanes[n]]
        v = [at(v_ref, n) for n in every]
        lhs = [jnp.concatenate([at(kap_ref, n), at(rt_ref, n)], axis=0) for n in every]
        ab = [_dot_nt(lhs[n], _bd_tile(at(bet_ref, n), bd)) for n in every]
        ak = [_dot_nt(lhs[n], _bd_tile(at(kt_ref, n), bd)) for n in every]
        p = [-jnp.where(strict, ab[n][0:c], 0.0) for n in every]
        t = [jnp.where(eye, 1.0, 0.0) + p[n] for n in every]
        for _ in range(int(math.log2(c)) - 1):
            pb = [x.astype(BF16) for x in p]
            p = [_dot(pb[n], _bd_tile(pb[n], bd)) for n in every]
            t = [t[n] + _dot(t[n].astype(BF16), _bd_tile(p[n].astype(BF16), bd)) for n in every]
        st = [s_ref[sq, gi] for sq, gi in chains]
        kr = [_dot_nt(lhs[n], st[n].astype(BF16)) for n in every]
        vbd = [_bd_tile(v[n], bd) for n in every]
        x = [kr[n][0:c] + _dot(jnp.where(strict, ak[n][0:c], 0.0).astype(BF16), vbd[n]) for n in every]
        ub = [_dot(t[n].astype(BF16), _bd_tile(x[n].astype(BF16), bd)).astype(BF16) for n in every]
        y = [kr[n][c:2 * c] + _dot(jnp.where(incl, ak[n][c:2 * c], 0.0).astype(BF16), vbd[n])
             - _dot(jnp.where(incl, ab[n][c:2 * c], 0.0).astype(BF16), _bd_tile(ub[n], bd)) for n in every]
        upd = [_dot_tn(jnp.concatenate([v[n], ub[n]], axis=0),
                       jnp.concatenate([at(ktc_ref, n), -at(betc_ref, n)], axis=0)) for n in every]
        mom = [_dot(jnp.concatenate([y[n], y[n] * y[n]], axis=0).astype(BF16), bd) * (1.0 / RWKV_HEAD_DIM)
               for n in every]
        for n, (sq, gi) in enumerate(chains):
            ec = ec_ref[sq, pl.ds(ci, 1), lanes[n]]
            s_ref[sq, gi] = st[n] * ec + jnp.where(bd_b, upd[n], 0.0)
            mean = mom[n][0:c]
            var = mom[n][c:2 * c] - mean * mean
            yn = (y[n] - mean) * lax.rsqrt(var + RWKV_LN_EPS) * lnw_ref[:, lanes[n]] + lnb_ref[:, lanes[n]]
            yn = (yn + at(bon_ref, n).astype(F32)) * at(g_ref, n).astype(F32)
            o_ref[sq, rows, lanes[n]] = yn.astype(o_ref.dtype)
        return carry

    lax.fori_loop(0, kap_ref.shape[1] // c, chunk, 0)


SEQ_PER_STEP = 8
REC_BLOCK = 128


def _seq_per_step(b):
    return max(n for n in range(1, SEQ_PER_STEP + 1) if b % n == 0)


def _per_chunk_spec(x, nsq):
    b, nc, w = x.shape
    cps = REC_BLOCK // CHUNK
    return x.reshape(b, nc // cps, cps, w), pl.BlockSpec((nsq, None, cps, w), lambda bi, i: (bi, i, 0, 0))


def _rwkv_chunk(ops, ec, ln_w, ln_b):
    b, s, w = ops[0].shape
    nsq = _seq_per_step(b)
    lblk = REC_BLOCK
    seq = pl.BlockSpec((nsq, lblk, w), lambda bi, i: (bi, i, 0))
    par = pl.BlockSpec((1, w), lambda bi, i: (0, 0))
    gw = HEADS_PER_PASS * RWKV_HEAD_DIM
    ec4, ec_spec = _per_chunk_spec(ec, nsq)
    return pl.pallas_call(
        _rwkv_chunk_kernel,
        grid=(b // nsq, s // lblk),
        in_specs=[seq] * 9 + [ec_spec, par, par],
        out_specs=seq,
        out_shape=jax.ShapeDtypeStruct((b, s, w), BF16),
        scratch_shapes=[pltpu.VMEM((nsq, w // gw, gw, gw), F32)],
        compiler_params=_cp("parallel", "arbitrary"),
        name="rwkv_chunk",
    )(*ops, ec4, ln_w.reshape(1, w), ln_b.reshape(1, w))


def _gdn_prep_kernel(x_ref, xp_ref, sm_ref, z_ref, cw_ref, alog_ref, dtb_ref, ltri_ref, utri_ref, same_ref, tot_ref,
                     q_ref, k_ref, kb_ref, vb_ref, kbe_ref, qg_ref, kg_ref, zs_ref, dec_ref, egl_ref):
    first = pl.program_id(1) == 0
    dh = GDN_HEAD_DIM
    w = MIX_W
    x = x_ref[...].astype(F32)
    ts = x.shape[0]
    prev8 = _prev_rows(xp_ref, first)
    acc = x * cw_ref[GDN_CONV - 1:GDN_CONV, :]
    for d in range(1, GDN_CONV):
        acc = acc + _shift_down(x, prev8, d) * cw_ref[GDN_CONV - 1 - d:GDN_CONV - d, :]
    act = acc * _sigmoid(acc)
    sm = sm_ref[...].astype(F32)
    gs = [-jnp.exp(alog_ref[:, h:h + 1]) * _softplus(sm[:, 16 + h:17 + h] + dtb_ref[:, h:h + 1])
          for h in range(GDN_HEADS)]
    gc = jnp.concatenate([jnp.broadcast_to(g, (ts, CHUNK)) for g in gs], axis=1)
    gc3 = _split3(gc)
    gamc = _sum01(ltri_ref[...], gc3)
    sufc = _sum01(utri_ref[...], gc3)
    t_in = lax.broadcasted_iota(jnp.int32, gc.shape, 0) % CHUNK
    s_in = lax.broadcasted_iota(jnp.int32, gc.shape, 1) % CHUNK
    gamr = _sum01(same_ref[...], jnp.where(t_in == s_in, gamc, 0.0))
    dec_ref[...] = jnp.exp(jnp.where(t_in >= s_in, gamc - gamr, NEG))
    eglc = jnp.exp(_sum01(tot_ref[...], gc3))[0:egl_ref.shape[0]]
    egl_ref[...] = jnp.concatenate([eglc[:, h * CHUNK:(h + 1) * CHUNK] for h in range(GDN_HEADS)
                                    for _ in range(dh // CHUNK)],
                                   axis=1)
    for h in range(GDN_HEADS):
        ls = slice(h * dh, (h + 1) * dh)
        q = act[:, h * dh:(h + 1) * dh]
        k = act[:, w + h * dh:w + (h + 1) * dh]
        v = act[:, 2 * w + h * dh:2 * w + (h + 1) * dh]
        q = q * lax.rsqrt(jnp.sum(q * q, axis=-1, keepdims=True) + EPS) * (dh ** -0.5)
        k = k * lax.rsqrt(jnp.sum(k * k, axis=-1, keepdims=True) + EPS)
        beta = _sigmoid(sm[:, 20 + h:21 + h])
        eg = jnp.exp(gamc[:, h * CHUNK:h * CHUNK + 1])
        es = jnp.exp(sufc[:, h * CHUNK:h * CHUNK + 1])
        kb = k * beta
        q_ref[:, ls] = q.astype(BF16)
        k_ref[:, ls] = k.astype(BF16)
        kb_ref[:, ls] = kb.astype(BF16)
        vb_ref[:, ls] = (v * beta).astype(BF16)
        kbe_ref[:, ls] = (kb * eg).astype(BF16)
        qg_ref[:, ls] = (q * eg).astype(BF16)
        kg_ref[:, ls] = (k * es).astype(BF16)
    z = z_ref[...].astype(F32)
    zs_ref[...] = (z * _sigmoid(z)).astype(BF16)


def _gdn_prep(proj3, conv_w, a_log, dt_bias):
    b, s, _ = proj3.shape
    w = MIX_W
    ts = PREP_TILE
    nrb = ts // PREV_ROWS
    nck = ts // CHUNK
    cw = HEADS_PER_PASS * CHUNK
    ltri, utri, same, tot = _chunk_consts(ts)
    full = lambda shp: pl.BlockSpec(shp, lambda bi, i: (0,) * len(shp))
    seq = pl.BlockSpec((None, ts, w), lambda bi, i: (bi, i, 0))
    return pl.pallas_call(
        _gdn_prep_kernel,
        grid=(b, s // ts),
        in_specs=[pl.BlockSpec((None, ts, 3 * w), lambda bi, i: (bi, i, C_GDN // (3 * w))),
                  pl.BlockSpec((None, PREV_ROWS, 3 * w),
                               lambda bi, i: (bi, jnp.maximum(i * nrb - 1, 0), C_GDN // (3 * w))),
                  pl.BlockSpec((None, ts, 128), lambda bi, i: (bi, i, C_SM0 // 128)),
                  pl.BlockSpec((None, ts, w), lambda bi, i: (bi, i, (C_GDN + 3 * w) // w)),
                  full((GDN_CONV, 3 * w)), full((1, GDN_HEADS)), full((1, GDN_HEADS)),
                  full((ts, ts)), full((ts, ts)), full((ts, ts)), full(tot.shape)],
        out_specs=[seq] * 8 + [pl.BlockSpec((None, ts, cw), lambda bi, i: (bi, i, 0)),
                               pl.BlockSpec((None, nck, w), lambda bi, i: (bi, i, 0))],
        out_shape=[jax.ShapeDtypeStruct((b, s, w), BF16)] * 8 + [jax.ShapeDtypeStruct((b, s, cw), F32),
                                                                 jax.ShapeDtypeStruct((b, s // CHUNK, w), F32)],
        compiler_params=_cp("parallel", "parallel"),
        name="gdn_prep",
    )(proj3, proj3, proj3, proj3, conv_w, a_log.reshape(1, GDN_HEADS), dt_bias.reshape(1, GDN_HEADS),
      ltri, utri, same, tot)


def _gdn_chunk_kernel(q_ref, k_ref, kb_ref, vb_ref, kbe_ref, qg_ref, kg_ref, zs_ref, dec_ref, egl_ref, nw_ref,
                      o_ref, s_ref):
    c = CHUNK
    dh = GDN_HEAD_DIM
    w = MIX_W
    pw = 2 * dh
    npair = w // pw
    nseq = q_ref.shape[0]

    @pl.when(pl.program_id(1) == 0)
    def _():
        s_ref[...] = jnp.zeros_like(s_ref)

    _, strict, eye = _cat_tri_masks()
    cw = HEADS_PER_PASS * c
    bd64 = jnp.where(_block_mask(cw, cw, c, c), 1.0, 0.0).astype(BF16)
    bdk = jnp.where(_block_mask(cw, w, c, dh), 1.0, 0.0).astype(BF16)
    bdp_b = _block_mask(pw, pw, dh, dh)

    seqs = range(nseq)
    pairs = [slice(p * pw, (p + 1) * pw) for p in range(npair)]

    def chunk(ci, carry):
        rows = pl.ds(pl.multiple_of(ci * c, c), c)
        lhs = [jnp.concatenate([kb_ref[sq, rows, :], q_ref[sq, rows, :]], axis=0) for sq in seqs]
        aq = [_dot_nt(lhs[sq], _bd_tile(k_ref[sq, rows, :], bdk)) for sq in seqs]
        dec = [dec_ref[sq, rows, :] for sq in seqs]
        p = [-jnp.where(strict, aq[sq][0:c] * dec[sq], 0.0) for sq in seqs]
        t = [jnp.where(eye, 1.0, 0.0) + p[sq] for sq in seqs]
        for _ in range(int(math.log2(c)) - 1):
            pb = [x.astype(BF16) for x in p]
            p = [_dot(pb[sq], _bd_tile(pb[sq], bd64)) for sq in seqs]
            t = [t[sq] + _dot(t[sq].astype(BF16), _bd_tile(p[sq].astype(BF16), bd64)) for sq in seqs]
        tb = [x.astype(BF16) for x in t]
        u = [_dot(tb[sq], _bd_tile(vb_ref[sq, rows, :], bdk)) for sq in seqs]
        wm = [_dot(tb[sq], _bd_tile(kbe_ref[sq, rows, :], bdk)).astype(BF16) for sq in seqs]
        st = [[s_ref[sq, pi] for pi in range(npair)] for sq in seqs]
        stb = [[x.astype(BF16) for x in st[sq]] for sq in seqs]
        ws = [jnp.concatenate([_dot(wm[sq][:, ps], stb[sq][pi]) for pi, ps in enumerate(pairs)], axis=1)
              for sq in seqs]
        vnb = [(u[sq] - ws[sq]).astype(BF16) for sq in seqs]
        qs = [jnp.concatenate([_dot(qg_ref[sq, rows, ps], stb[sq][pi]) for pi, ps in enumerate(pairs)], axis=1)
              for sq in seqs]
        o = [qs[sq] + _dot((aq[sq][c:2 * c] * dec[sq]).astype(BF16), _bd_tile(vnb[sq], bdk)) for sq in seqs]
        upd = [[_dot_tn(kg_ref[sq, rows, ps], vnb[sq][:, ps]) for ps in pairs] for sq in seqs]
        for sq in seqs:
            egl = egl_ref[sq, pl.ds(ci, 1), :]
            for pi, ps in enumerate(pairs):
                s_ref[sq, pi] = st[sq][pi] * egl[:, ps] + jnp.where(bdp_b, upd[sq][pi], 0.0)
            for h in range(GDN_HEADS):
                ls = slice(h * dh, (h + 1) * dh)
                oh = o[sq][:, ls]
                ms = jnp.mean(oh * oh, axis=-1, keepdims=True)
                o_ref[sq, rows, ls] = (oh * lax.rsqrt(ms + EPS) * nw_ref[...]
                                       * zs_ref[sq, rows, ls].astype(F32)).astype(o_ref.dtype)
        return carry

    lax.fori_loop(0, q_ref.shape[1] // c, chunk, 0)


def _gdn_chunk(ops, dec, egl, norm_w):
    b, s, w = ops[0].shape
    dh = GDN_HEAD_DIM
    nsq = _seq_per_step(b)
    lblk = REC_BLOCK
    seq = pl.BlockSpec((nsq, lblk, w), lambda bi, i: (bi, i, 0))
    egl, egl_spec = _per_chunk_spec(egl, nsq)
    return pl.pallas_call(
        _gdn_chunk_kernel,
        grid=(b // nsq, s // lblk),
        in_specs=[seq] * 8 + [pl.BlockSpec((nsq, lblk, dec.shape[2]), lambda bi, i: (bi, i, 0)), egl_spec,
                              pl.BlockSpec((1, dh), lambda bi, i: (0, 0))],
        out_specs=seq,
        out_shape=jax.ShapeDtypeStruct((b, s, w), BF16),
        scratch_shapes=[pltpu.VMEM((nsq, w // (2 * dh), 2 * dh, 2 * dh), F32)],
        compiler_params=_cp("parallel", "arbitrary"),
        name="gdn_chunk",
    )(*ops, dec, egl, norm_w.reshape(1, dh))


def _merge_kernel(x_ref, oa_ref, ob_ref, oc_ref, ga_ref, gb_ref, gc_ref, wb_ref, wo_ref, o_ref):
    merged = None
    for j, (br, gate) in enumerate(((oa_ref, ga_ref), (ob_ref, gb_ref), (oc_ref, gc_ref))):
        t = _sigmoid(gate[...].astype(F32)) * _dot(br[...], wb_ref[j])
        merged = t if merged is None else merged + t
    o_ref[...] = x_ref[...] + _dot(merged.astype(BF16), wo_ref[...])


def _merge(x2, o_a, o_b, o_c, proj, w_branch, w_out, tm=512):
    t, d = x2.shape
    w = MIX_W
    tm = min(tm, t)
    br = pl.BlockSpec((tm, w), lambda i: (i, 0))
    gate = lambda j: pl.BlockSpec((tm, d), lambda i: (i, C_GATE // d + j))
    return pl.pallas_call(
        _merge_kernel,
        grid=(t // tm,),
        in_specs=[pl.BlockSpec((tm, d), lambda i: (i, 0)), br, br, br, gate(0), gate(1), gate(2),
                  pl.BlockSpec((3, w, d), lambda i: (0, 0, 0)),
                  pl.BlockSpec((d, d), lambda i: (0, 0))],
        out_specs=pl.BlockSpec((tm, d), lambda i: (i, 0)),
        out_shape=jax.ShapeDtypeStruct((t, d), F32),
        compiler_params=_cp("parallel"),
        name="merge",
    )(x2, o_a.reshape(t, w), o_b.reshape(t, w), o_c.reshape(t, w), proj, proj, proj,
      w_branch.astype(BF16), w_out.astype(BF16))


def _ffn_kernel(x_ref, g_ref, w1_ref, w2_ref, o_ref, hn_ref):
    j = pl.program_id(1)

    @pl.when(j == 0)
    def _():
        x = x_ref[...]
        ms = jnp.mean(x * x, axis=-1, keepdims=True)
        hn_ref[...] = (x * lax.rsqrt(ms + EPS) * g_ref[...]).astype(BF16)
        o_ref[...] = x

    h1 = jnp.maximum(_dot(hn_ref[...], w1_ref[...]), 0.0)
    o_ref[...] += _dot((h1 * h1).astype(BF16), w2_ref[...])


def _ffn(x2, g, w1, w2, tm=1024, tf=512):
    t, d = x2.shape
    f = w1.shape[1]
    tm = min(tm, t)
    return pl.pallas_call(
        _ffn_kernel,
        grid=(t // tm, f // tf),
        in_specs=[pl.BlockSpec((tm, d), lambda i, j: (i, 0)),
                  pl.BlockSpec((1, d), lambda i, j: (0, 0)),
                  pl.BlockSpec((d, tf), lambda i, j: (0, j)),
                  pl.BlockSpec((tf, d), lambda i, j: (j, 0))],
        out_specs=pl.BlockSpec((tm, d), lambda i, j: (i, 0)),
        out_shape=jax.ShapeDtypeStruct((t, d), F32),
        scratch_shapes=[pltpu.VMEM((tm, d), BF16)],
        compiler_params=_cp("parallel", "arbitrary"),
        name="ffn",
    )(x2, g.reshape(1, d), w1.astype(BF16), w2.astype(BF16))


def kernel(x, norm_mix_g, w_in, nsa_q_norm, nsa_k_norm, nsa_cmp_pos, nsa_cmp_w1, nsa_cmp_w2, rwkv_mu, rwkv_w0, rwkv_w_up, rwkv_a0, rwkv_a_up, rwkv_g_up, rwkv_k_k, rwkv_k_a, rwkv_r_k, rwkv_ln_w, rwkv_ln_b, rwkv_v0, rwkv_vres_down, rwkv_vres_up, gdn_conv_w, gdn_a_log, gdn_dt_bias, gdn_norm_w, w_branch, w_out, norm_ffn_g, w_ff1, w_ff2):
    b, s, d = x.shape
    depth = w_in.shape[0]
    perm = jnp.asarray(_proj_perm())
    x2 = x.reshape(b * s, d)
    v_first = None
    for i in range(depth):
        vd = rwkv_vres_down[i - 1] if i > 0 else jnp.zeros((d, 32), F32)
        w_ext = jnp.concatenate([w_in[i], vd, jnp.zeros((d, 1), F32)], axis=1)
        w_all = jnp.take(w_ext, perm, axis=1).astype(BF16)
        proj = _rms_matmul(x2, norm_mix_g[i], w_all)
        proj3 = proj.reshape(b, s, N_PROJ)

        o_a = _nsa(proj3, nsa_q_norm[i], nsa_k_norm[i], nsa_cmp_pos[i], nsa_cmp_w1[i], nsa_cmp_w2[i])

        vres = None if i == 0 else (rwkv_v0[i - 1], rwkv_vres_up[i - 1])
        rw = _rwkv_prep(proj3, rwkv_mu[i], rwkv_w0[i], rwkv_w_up[i], rwkv_a0[i], rwkv_a_up[i], rwkv_g_up[i],
                        rwkv_k_k[i], rwkv_k_a[i], rwkv_r_k[i], v_first, vres)
        if i == 0:
            v_first = rw[10]
        o_b = _rwkv_chunk(rw[:9], rw[9], rwkv_ln_w[i], rwkv_ln_b[i])

        gd = _gdn_prep(proj3, gdn_conv_w[i], gdn_a_log[i], gdn_dt_bias[i])
        o_c = _gdn_chunk(gd[:8], gd[8], gd[9], gdn_norm_w[i])

        x2 = _merge(x2, o_a, o_b, o_c, proj, w_branch[i], w_out[i])
        x2 = _ffn(x2, norm_ffn_g[i], w_ff1[i], w_ff2[i])
    return x2.reshape(b, s, d)
```

```python
import functools
import math

import numpy as np
import jax
import jax.numpy as jnp
from jax import lax
from jax.experimental import pallas as pl
from jax.experimental.pallas import tpu as pltpu

F32 = jnp.float32
BF16 = jnp.bfloat16

D_MODEL = 1024
MIX_W = 512
NSA_HEAD_DIM = 64
NSA_KV_GROUPS = 2
NSA_HPG = 4
NSA_CMP_BLOCK = 32
NSA_CMP_STRIDE = 16
NSA_CMP_HIDDEN = 256
NSA_SEL_BLOCK = 64
NSA_N_SEL = 16
NSA_WINDOW = 512
Q_BLOCK = 128
NSA_IN = 1304
RWKV_HEADS = 8
RWKV_HEAD_DIM = 64
RWKV_LN_EPS = 64e-5
RWKV_IN = 1792
GDN_HEADS = 4
GDN_HEAD_DIM = 128
GDN_CONV = 4
GDN_IN = 2056
D_IN = 8224
D_FF = 4096
EPS = 1e-6
NEG = -1e30
FORCE = 1e4

C_KV = 0
C_Q = 768
C_SM0 = 1280
C_SM1 = 1408
SM_GDN_ALPHA = 16
SM_GDN_BETA = 20
SM_VRES = 32
VRES_RANK = 32
C_RKV = 1536
C_GDN = 3072
C_GATE = 5120
C_LR = 8192
N_PROJ = 8448
VMEM_LIMIT = 48 * 1024 * 1024


def _proj_perm():
    zero = D_IN + VRES_RANK
    perm = np.full((N_PROJ,), zero, np.int32)
    perm[C_KV:C_KV + 768] = np.arange(512, 1280)
    perm[C_Q:C_Q + 512] = np.arange(0, 512)
    perm[C_SM0:C_SM0 + 12] = 1280 + np.arange(12)
    perm[C_SM1:C_SM1 + 12] = 1292 + np.arange(12)
    rw = NSA_IN
    gd = NSA_IN + RWKV_IN
    gt = gd + GDN_IN
    perm[C_SM0 + SM_GDN_ALPHA:C_SM0 + SM_GDN_ALPHA + GDN_HEADS] = gd + 2048 + np.arange(GDN_HEADS)
    perm[C_SM0 + SM_GDN_BETA:C_SM0 + SM_GDN_BETA + GDN_HEADS] = gd + 2048 + GDN_HEADS + np.arange(GDN_HEADS)
    perm[C_SM0 + SM_VRES:C_SM0 + SM_VRES + VRES_RANK] = D_IN + np.arange(VRES_RANK)
    perm[C_RKV:C_RKV + 1536] = rw + np.arange(1536)
    perm[C_LR:C_LR + 256] = rw + 1536 + np.arange(256)
    perm[C_GDN:C_GDN + 2048] = gd + np.arange(2048)
    perm[C_GATE:C_GATE + 3072] = gt + np.arange(3072)
    return perm


def _cp(*sem):
    return pltpu.CompilerParams(dimension_semantics=sem, vmem_limit_bytes=VMEM_LIMIT)


def _dot(a, b):
    return jnp.dot(a, b, preferred_element_type=F32)


def _dot_nt(a, b):
    return lax.dot_general(a, b, (((1,), (1,)), ((), ())), preferred_element_type=F32)


def _dot_tn(a, b):
    return lax.dot_general(a, b, (((0,), (0,)), ((), ())), preferred_element_type=F32)


def _split(a):
    hi = a.astype(BF16)
    lo = (a - hi.astype(F32)).astype(BF16)
    return hi, lo


def _mm3(a, b):
    ah, al = _split(a)
    bh, bl = _split(b)
    return _dot(ah, bh) + (_dot(ah, bl) + _dot(al, bh))


def _split3(x):
    hi = x.astype(BF16)
    r1 = x - hi.astype(F32)
    mid = r1.astype(BF16)
    return hi, mid, (r1 - mid.astype(F32)).astype(BF16)


def _sum01(m, x):
    hi, mid, lo = x if isinstance(x, tuple) else _split3(x)
    return _dot(m, hi) + (_dot(m, mid) + _dot(m, lo))


def _segsum_left(m, x):
    hi, lo = _split(x)
    return _dot(m, hi) + _dot(m, lo)


def _segsum(x, seg):
    hi, lo = _split(x)
    return _dot(hi, seg) + _dot(lo, seg)


CHUNK = 64
HEADS_PER_PASS = 4


def _chunk_consts(ts):
    r = np.arange(ts)
    same = (r[:, None] // CHUNK) == (r[None, :] // CHUNK)
    ltri = same & (r[:, None] >= r[None, :])
    utri = same & (r[:, None] < r[None, :])
    nck = ts // CHUNK
    tot = np.zeros((max(nck, 8), ts), bool)
    tot[:nck] = (r[None, :] // CHUNK) == np.arange(nck)[:, None]
    return [jnp.asarray(m, BF16) for m in (ltri, utri, same, tot)]


def _block_mask(rows, cols, rblk, cblk):
    r = lax.broadcasted_iota(jnp.int32, (rows, cols), 0) // rblk
    c = lax.broadcasted_iota(jnp.int32, (rows, cols), 1) // cblk
    return r == c


def _cat_tri_masks():
    w = HEADS_PER_PASS * CHUNK
    t = lax.broadcasted_iota(jnp.int32, (CHUNK, w), 0)
    s = lax.broadcasted_iota(jnp.int32, (CHUNK, w), 1) % CHUNK
    return t >= s, t > s, t == s


def _bd_tile(y, bd):
    return jnp.concatenate([y] * HEADS_PER_PASS, axis=0) * bd


def _softplus(z):
    return jnp.maximum(z, 0.0) + jnp.log1p(jnp.exp(-jnp.abs(z)))


def _sigmoid(z):
    return 1.0 / (1.0 + jnp.exp(-z))


def _rms_matmul_kernel(x_ref, g_ref, w_ref, o_ref, hn_ref):
    @pl.when(pl.program_id(1) == 0)
    def _():
        x = x_ref[...]
        ms = jnp.mean(x * x, axis=-1, keepdims=True)
        hn_ref[...] = (x * lax.rsqrt(ms + EPS) * g_ref[...]).astype(BF16)

    o_ref[...] = _dot(hn_ref[...], w_ref[...]).astype(o_ref.dtype)


def _rms_matmul(x2, g, w, tm=1024, tn=768):
    t, d = x2.shape
    n = w.shape[1]
    tm = min(tm, t)
    return pl.pallas_call(
        _rms_matmul_kernel,
        grid=(t // tm, n // tn),
        in_specs=[pl.BlockSpec((tm, d), lambda i, j: (i, 0)),
                  pl.BlockSpec((1, d), lambda i, j: (0, 0)),
                  pl.BlockSpec((d, tn), lambda i, j: (0, j))],
        out_specs=pl.BlockSpec((tm, tn), lambda i, j: (i, j)),
        out_shape=jax.ShapeDtypeStruct((t, n), BF16),
        scratch_shapes=[pltpu.VMEM((tm, d), BF16)],
        compiler_params=_cp("parallel", "arbitrary"),
        name="rms_proj",
    )(x2, g.reshape(1, d), w)


def _nsa_kvprep_kernel(kv_ref, kg_ref, kc_ref, vc_ref, ks_ref, vs_ref, kw_ref, vw_ref):
    outs = (kc_ref, vc_ref, ks_ref, vs_ref, kw_ref, vw_ref)
    for j in range(6):
        for g in range(NSA_KV_GROUPS):
            lo = j * 128 + g * NSA_HEAD_DIM
            piece = kv_ref[:, lo:lo + NSA_HEAD_DIM].astype(F32)
            if j in (2, 4):
                gain = kg_ref[j // 2:j // 2 + 1, :]
                ms = jnp.mean(piece * piece, axis=-1, keepdims=True)
                piece = piece * lax.rsqrt(ms + EPS) * gain
            outs[j][g] = piece.astype(outs[j].dtype)


def _nsa_kvprep(proj3, k_g, ts=1024):
    b, s, _ = proj3.shape
    g, dh = NSA_KV_GROUPS, NSA_HEAD_DIM
    out_spec = lambda wd: pl.BlockSpec((None, g, ts, wd), lambda bi, i: (bi, 0, i, 0))
    shp = lambda dt, wd=dh: jax.ShapeDtypeStruct((b, g, s, wd), dt)
    return pl.pallas_call(
        _nsa_kvprep_kernel,
        grid=(b, s // ts),
        in_specs=[pl.BlockSpec((None, ts, 768), lambda bi, i: (bi, i, C_KV // 768)),
                  pl.BlockSpec((3, dh), lambda bi, i: (0, 0))],
        out_specs=[out_spec(dh)] * 6,
        out_shape=[shp(F32), shp(F32), shp(BF16), shp(BF16), shp(BF16), shp(BF16)],
        compiler_params=_cp("parallel", "parallel"),
        name="nsa_kvprep",
    )(proj3, k_g)


def _nsa_compress_kernel(x_ref, w1_ref, pos_ref, w2_ref, g_ref, o_ref, *, normalize):
    x = x_ref[...].astype(BF16)
    half = x.shape[1]
    a = _dot(x, w1_ref[0:half, :])
    bm = _dot(x, w1_ref[half:2 * half, :])
    bias = _dot(pos_ref[...].astype(BF16), w1_ref[...])[0:1, :]
    n = x.shape[0]
    hid = a + pltpu.roll(bm, n - 1, 0) + bias
    hid = hid * _sigmoid(hid)
    out = _dot(hid.astype(BF16), w2_ref[...])
    if normalize:
        ms = jnp.mean(out * out, axis=-1, keepdims=True)
        out = out * lax.rsqrt(ms + EPS) * g_ref[...]
    o_ref[...] = out.astype(o_ref.dtype)


def _nsa_compress(xh, w1, pos, w2, gain, normalize):
    bg, n, half = xh.shape
    dh = NSA_HEAD_DIM
    pos8 = jnp.broadcast_to(pos.reshape(1, 2 * half), (8, 2 * half))
    return pl.pallas_call(
        functools.partial(_nsa_compress_kernel, normalize=normalize),
        grid=(bg,),
        in_specs=[pl.BlockSpec((None, n, half), lambda i: (i, 0, 0)),
                  pl.BlockSpec((2 * half, NSA_CMP_HIDDEN), lambda i: (0, 0)),
                  pl.BlockSpec((8, 2 * half), lambda i: (0, 0)),
                  pl.BlockSpec((NSA_CMP_HIDDEN, dh), lambda i: (0, 0)),
                  pl.BlockSpec((1, dh), lambda i: (0, 0))],
        out_specs=pl.BlockSpec((None, n, dh), lambda i: (i, 0, 0)),
        out_shape=jax.ShapeDtypeStruct((bg, n, dh), BF16),
        compiler_params=_cp("parallel"),
        name="nsa_compress",
    )(xh, w1.astype(BF16), pos8, w2.astype(BF16), gain.reshape(1, dh))


SEL_TILE = 256
STRIP = 128
LOG2E = 1.4426950408889634


def _softmax_cols(s, valid):
    s = jnp.where(valid, s, NEG)
    p = jnp.exp2(s - jnp.max(s, axis=0, keepdims=True))
    return p, jnp.sum(p, axis=0, keepdims=True)


def _nsa_attn_t_kernel(q_ref, sm_ref, kc_ref, vct_ref, ks_ref, vst_ref, kw_ref, vwt_ref, c2s_ref, qg_ref, o_ref,
                       sb_ref, acc_ref, al_ref, m_ref, l_ref, sa_ref, sb2_ref, pa_ref, pb_ref, *, n_sel):
    dh, hg, qb = NSA_HEAD_DIM, NSA_HPG, Q_BLOCK
    cols = hg * qb
    tk = SEL_TILE
    blk = pl.program_id(2)
    start = blk * qb

    xt = jnp.transpose(q_ref[...].astype(F32))
    qs = []
    for h in range(hg):
        xh = xt[h * dh:(h + 1) * dh, :]
        ms = jnp.mean(xh * xh, axis=0, keepdims=True)
        qs.append(xh * lax.rsqrt(ms + EPS) * qg_ref[...])
    qt = jnp.concatenate(qs, axis=1).astype(BF16)
    tq_l = start + lax.broadcasted_iota(jnp.int32, (1, qb), 1)
    heads = [slice(h * qb, (h + 1) * qb) for h in range(hg)]

    t_col = jnp.concatenate([tq_l] * hg, axis=1)

    n_cmp = kc_ref.shape[0]
    cvalid = (lax.broadcasted_iota(jnp.int32, (n_cmp, 1), 0) * NSA_CMP_STRIDE + (NSA_CMP_BLOCK - 1)) <= t_col
    p, l = _softmax_cols(_dot(kc_ref[...], qt), cvalid)
    p = p * jnp.where(t_col >= NSA_CMP_BLOCK - 1, 1.0 / l, 0.0)
    o_c = _dot(vct_ref[...], p.astype(BF16))
    psum = p[:, heads[0]]
    for cs in heads[1:]:
        psum = psum + p[:, cs]
    imp_t = _segsum_left(c2s_ref[...], psum)

    n_slc = imp_t.shape[0]
    jr = lax.broadcasted_iota(jnp.int32, (n_slc, 1), 0)
    cur = jnp.right_shift(tq_l, int(math.log2(NSA_SEL_BLOCK)))
    forced = (jr == 0) | (jr == cur) | (jr == cur - 1)
    causal = jr * NSA_SEL_BLOCK <= tq_l
    val = jnp.where(forced, -jnp.inf, jnp.where(causal, imp_t, -FORCE))
    jrf = jr.astype(F32)

    wlen = NSA_WINDOW + qb
    base = pl.multiple_of(jnp.maximum(start - NSA_WINDOW, 0), qb)
    dist = t_col - (base + lax.broadcasted_iota(jnp.int32, (wlen, 1), 0))
    in_window = lax.bitcast_convert_type(dist, jnp.uint32) < jnp.uint32(NSA_WINDOW)
    p, l = _softmax_cols(_dot(kw_ref[pl.ds(base, wlen), :], qt), in_window)
    o_w = _dot(vwt_ref[:, pl.ds(base, wlen)], p.astype(BF16)) * (1.0 / l)

    for _ in range(n_sel - 3):
        mx = jnp.max(val, axis=0, keepdims=True)
        idx = jnp.min(jnp.where(val == mx, jrf, float(n_slc)), axis=0, keepdims=True)
        val = jnp.where(jrf == idx, -jnp.inf, val)
    sb_ref[...] = jnp.where(val == -jnp.inf, 0.0, NEG)

    bpt = tk // NSA_SEL_BLOCK
    last = start // tk
    n_pairs = (last + 1) // 2
    strips = [(slice(c, c + STRIP), c % qb) for c in range(0, cols, STRIP)]

    def key_off(i):
        return pl.multiple_of(jnp.minimum(i, last) * tk, tk)

    def scores(i):
        return _dot(ks_ref[pl.ds(key_off(i), tk), :], qt)

    def step(tile, prev_tile, next_tile, s_cur, s_nxt, p_cur, p_prev, first):
        s_nxt[...] = scores(next_tile)
        if not first:
            pv = _dot(vst_ref[:, pl.ds(key_off(prev_tile), tk)], p_prev[...])
        rows = [sb_ref[pl.ds(jnp.minimum(tile, last) * bpt + j, 1), :] for j in range(bpt)]
        if first:
            kpos = tile * tk + lax.broadcasted_iota(jnp.int32, (tk, 1), 0)
        else:
            rows = [jnp.where(tile < last, r, NEG) for r in rows]
        for cs, off in strips:
            ts_ = slice(off, off + STRIP)
            sh = jnp.concatenate([s_cur[j * NSA_SEL_BLOCK:(j + 1) * NSA_SEL_BLOCK, cs] + rows[j][:, ts_]
                                  for j in range(bpt)], axis=0)
            if first:
                sh = jnp.where(kpos <= tq_l[:, ts_], sh, NEG)
            m_old = m_ref[:, cs]
            m_new = jnp.maximum(m_old, jnp.max(sh, axis=0, keepdims=True))
            alpha = jnp.exp2(m_old - m_new)
            p = jnp.exp2(sh - m_new)
            p_cur[:, cs] = p.astype(BF16)
            m_ref[:, cs] = m_new
            l_ref[:, cs] = alpha * l_ref[:, cs] + jnp.sum(p, axis=0, keepdims=True)
            if not first:
                acc_ref[:, cs] = al_ref[:, cs] * acc_ref[:, cs] + pv[:, cs]
            al_ref[:, cs] = alpha

    m_ref[...] = jnp.full((1, cols), NEG, F32)
    l_ref[...] = jnp.zeros((1, cols), F32)
    acc_ref[...] = jnp.zeros((dh, cols), F32)
    sa_ref[...] = scores(last)
    step(last, last, 0, sa_ref, sb2_ref, pa_ref, pb_ref, True)

    def sel_body(j, carry):
        t0 = 2 * j
        step(t0, jnp.where(j == 0, last, t0 - 1), t0 + 1, sb2_ref, sa_ref, pb_ref, pa_ref, False)
        step(t0 + 1, t0, t0 + 2, sa_ref, sb2_ref, pa_ref, pb_ref, False)
        return carry

    lax.fori_loop(0, n_pairs, sel_body, 0)
    prev = jnp.where(n_pairs == 0, last, 2 * n_pairs - 1)
    acc = al_ref[...] * acc_ref[...] + _dot(vst_ref[:, pl.ds(key_off(prev), tk)], pa_ref[...])
    o_s = acc * (1.0 / l_ref[...])

    gt = jnp.transpose(_sigmoid(sm_ref[...].astype(F32)))
    outs = []
    for h, cs in enumerate(heads):
        outs.append(gt[3 * h:3 * h + 1, :] * o_c[:, cs] + gt[3 * h + 1:3 * h + 2, :] * o_s[:, cs]
                    + gt[3 * h + 2:3 * h + 3, :] * o_w[:, cs])
    o_ref[...] = jnp.transpose(jnp.concatenate(outs, axis=0)).astype(o_ref.dtype)


def _nsa(proj3, q_g, k_g, cmp_pos, cmp_w1, cmp_w2):
    b, s, _ = proj3.shape
    g, dh, qb = NSA_KV_GROUPS, NSA_HEAD_DIM, Q_BLOCK
    kc_raw, vc_raw, ks, vs, kw, vw = _nsa_kvprep(proj3, k_g, ts=min(1024, s))
    n_half = s // NSA_CMP_STRIDE
    half = NSA_CMP_STRIDE * dh
    kc = _nsa_compress(kc_raw.reshape(b * g, n_half, half), cmp_w1[0], cmp_pos[0], cmp_w2[0], k_g[0], True)
    vc = _nsa_compress(vc_raw.reshape(b * g, n_half, half), cmp_w1[1], cmp_pos[1], cmp_w2[1], k_g[0], False)
    kc = kc.reshape(b, g, n_half, dh)
    vc = vc.reshape(b, g, n_half, dh)

    n_slc = s // NSA_SEL_BLOCK
    n_sel = min(NSA_N_SEL, n_slc)
    cmp_start = np.arange(n_half) * NSA_CMP_STRIDE
    slc_start = np.arange(n_slc) * NSA_SEL_BLOCK
    overlap = np.clip(np.minimum(cmp_start[:, None] + NSA_CMP_BLOCK, slc_start[None, :] + NSA_SEL_BLOCK)
                      - np.maximum(cmp_start[:, None], slc_start[None, :]), 0, None)
    c2s = (overlap / NSA_CMP_BLOCK).T

    k_spec = lambda n, wd=dh: pl.BlockSpec((None, None, n, wd), lambda bi, gi, i: (bi, gi, 0, 0))
    vt_spec = lambda n: pl.BlockSpec((None, None, dh, n), lambda bi, gi, i: (bi, gi, 0, 0))
    tr = lambda v: jnp.swapaxes(v, 2, 3)
    cols = NSA_HPG * qb
    q_gain = jnp.broadcast_to((q_g * (dh ** -0.5 * LOG2E)).reshape(dh, 1), (dh, qb))
    return pl.pallas_call(
        functools.partial(_nsa_attn_t_kernel, n_sel=n_sel),
        grid=(b, g, s // qb),
        in_specs=[pl.BlockSpec((None, qb, 256), lambda bi, gi, i: (bi, i, C_Q // 256 + gi)),
                  pl.BlockSpec((None, qb, 128), lambda bi, gi, i: (bi, i, C_SM0 // 128 + gi)),
                  k_spec(n_half), vt_spec(n_half), k_spec(s), vt_spec(s), k_spec(s), vt_spec(s),
                  pl.BlockSpec((n_slc, n_half), lambda bi, gi, i: (0, 0)),
                  pl.BlockSpec((dh, qb), lambda bi, gi, i: (0, 0))],
        out_specs=pl.BlockSpec((None, qb, 256), lambda bi, gi, i: (bi, i, gi)),
        out_shape=jax.ShapeDtypeStruct((b, s, MIX_W), BF16),
        scratch_shapes=([pltpu.VMEM((n_slc, qb), F32), pltpu.VMEM((dh, cols), F32),
                         pltpu.VMEM((1, cols), F32), pltpu.VMEM((1, cols), F32), pltpu.VMEM((1, cols), F32)]
                        + [pltpu.VMEM((SEL_TILE, cols), F32)] * 2 + [pltpu.VMEM((SEL_TILE, cols), BF16)] * 2),
        compiler_params=_cp("parallel", "parallel", "arbitrary"),
        name="nsa_attn",
    )(proj3, proj3, kc, tr(vc), ks, tr(vs), kw, tr(vw), jnp.asarray(c2s, BF16), q_gain)


def _prev_rows(prev_ref, first):
    return jnp.where(first, 0.0, prev_ref[PREV_ROWS - 8:PREV_ROWS, :].astype(F32))


def _shift_down(x, prev8, d):
    rolled = pltpu.roll(x, d, 0)
    row8 = lax.broadcasted_iota(jnp.int32, prev8.shape, 0)
    top = jnp.where(row8 < d, pltpu.roll(prev8, d, 0), rolled[0:8])
    return jnp.concatenate([top, rolled[8:]], axis=0)


PREP_TILE = 512
PREV_ROWS = 16


def _rwkv_prep_kernel(*refs, has_vres):
    (rkv_ref, rkvp_ref, lr_ref, lrp_ref, mu1_ref, mu2_ref, w0_ref, wup_ref, a0_ref, aup_ref, gup_ref,
     kkp_ref, kap_ref, rkp_ref, seg_ref, ltri_ref, utri_ref, tot_ref) = refs[:18]
    if has_vres:
        sm_ref, vf_ref, v0_ref, vu_ref = refs[18:22]
        outs = refs[22:]
    else:
        outs = refs[18:]
    kapo_ref, beto_ref, kto_ref, rto_ref, ktc_ref, betc_ref, vo_ref, bon_ref, g_ref, ec_ref = outs[:10]
    first = pl.program_id(1) == 0
    w = MIX_W
    c = rkv_ref[...].astype(F32)
    c = c + (_shift_down(c, _prev_rows(rkvp_ref, first), 1) - c) * mu1_ref[...]
    lr = lr_ref[...].astype(F32)
    lr = lr + (_shift_down(lr, _prev_rows(lrp_ref, first), 1) - lr) * mu2_ref[...]
    r, k, v = c[:, 0:w], c[:, w:2 * w], c[:, 2 * w:3 * w]
    wd, ad, gd = lr[:, 0:64], lr[:, 64:128], lr[:, 128:256]
    wlog = -_softplus(-(w0_ref[...] + _mm3(jnp.tanh(wd), wup_ref[...]))) - 0.5
    lw = -jnp.exp(wlog)
    a = _sigmoid(a0_ref[...] + _mm3(ad, aup_ref[...]))
    g_ref[...] = _mm3(_sigmoid(gd), gup_ref[...]).astype(g_ref.dtype)
    if has_vres:
        mix = _sigmoid(v0_ref[...] + _mm3(sm_ref[:, SM_VRES:SM_VRES + VRES_RANK].astype(F32), vu_ref[...]))
        v = v + (vf_ref[...] - v) * mix
    else:
        outs[10][...] = v
    seg = seg_ref[...]
    kk = k * kkp_ref[...]
    kk = kk * lax.rsqrt(_segsum(kk * kk, seg) + EPS)
    k = k * (1.0 + (a - 1.0) * kap_ref[...])
    kka = kk * a
    lw3 = _split3(lw)
    lcum = _sum01(ltri_ref[...], lw3)
    e_suf = jnp.exp(_sum01(utri_ref[...], lw3))
    e_inv = jnp.exp(-lcum)
    kapo_ref[...] = (kk * jnp.exp(lcum - lw)).astype(BF16)
    beto_ref[...] = (kka * e_inv).astype(BF16)
    kto_ref[...] = (k * e_inv).astype(BF16)
    rto_ref[...] = (r * jnp.exp(lcum)).astype(BF16)
    ktc_ref[...] = (k * e_suf).astype(BF16)
    betc_ref[...] = (kka * e_suf).astype(BF16)
    vo_ref[...] = v.astype(BF16)
    bon_ref[...] = (_segsum(r * k * rkp_ref[...], seg) * v).astype(BF16)
    ec_ref[...] = jnp.exp(_sum01(tot_ref[...], lw3))[0:ec_ref.shape[0]]


def _rwkv_prep(proj3, mu, w0, w_up, a0, a_up, g_up, k_k, k_a, r_k, v_first, vres):
    b, s, _ = proj3.shape
    w = MIX_W
    has_vres = vres is not None
    ts = PREP_TILE
    nrb = ts // PREV_ROWS
    nck = ts // CHUNK
    cur = lambda wd, cb: pl.BlockSpec((None, ts, wd), lambda bi, i: (bi, i, cb))
    prev = lambda wd, cb: pl.BlockSpec((None, PREV_ROWS, wd), lambda bi, i: (bi, jnp.maximum(i * nrb - 1, 0), cb))
    full = lambda shp: pl.BlockSpec(shp, lambda bi, i: (0,) * len(shp))
    ltri, utri, _, tot = _chunk_consts(ts)
    hd = np.arange(w) // RWKV_HEAD_DIM
    seg = jnp.asarray(hd[:, None] == hd[None, :], BF16)
    in_specs = [cur(1536, C_RKV // 1536), prev(1536, C_RKV // 1536), cur(256, C_LR // 256), prev(256, C_LR // 256),
                full((1, 1536)), full((1, 256)), full((1, w)), full((64, w)), full((1, w)), full((64, w)),
                full((128, w)), full((1, w)), full((1, w)), full((1, w)), full((w, w)), full((ts, ts)),
                full((ts, ts)), full(tot.shape)]
    args = [proj3, proj3, proj3, proj3, mu[:1536].reshape(1, 1536), mu[1536:].reshape(1, 256), w0.reshape(1, w),
            w_up, a0.reshape(1, w), a_up, g_up, k_k.reshape(1, w), k_a.reshape(1, w), r_k.reshape(1, w),
            seg, ltri, utri, tot]
    if has_vres:
        v0, vu = vres
        in_specs += [cur(128, C_SM0 // 128), pl.BlockSpec((None, ts, w), lambda bi, i: (bi, i, 0)),
                     full((1, w)), full((VRES_RANK, w))]
        args += [proj3, v_first, v0.reshape(1, w), vu]
    seq = pl.BlockSpec((None, ts, w), lambda bi, i: (bi, i, 0))
    out_specs = [seq] * 9 + [pl.BlockSpec((None, nck, w), lambda bi, i: (bi, i, 0))]
    out_shape = [jax.ShapeDtypeStruct((b, s, w), BF16)] * 9 + [jax.ShapeDtypeStruct((b, s // CHUNK, w), F32)]
    if not has_vres:
        out_specs.append(seq)
        out_shape.append(jax.ShapeDtypeStruct((b, s, w), F32))
    return pl.pallas_call(
        functools.partial(_rwkv_prep_kernel, has_vres=has_vres),
        grid=(b, s // ts),
        in_specs=in_specs,
        out_specs=out_specs,
        out_shape=out_shape,
        compiler_params=_cp("parallel", "parallel"),
        name="rwkv_prep",
    )(*args)


def _rwkv_chunk_kernel(kap_ref, bet_ref, kt_ref, rt_ref, ktc_ref, betc_ref, v_ref, bon_ref, g_ref, ec_ref,
                       lnw_ref, lnb_ref, o_ref, s_ref):
    c = CHUNK
    gw = HEADS_PER_PASS * RWKV_HEAD_DIM
    nseq = kap_ref.shape[0]
    ngrp = kap_ref.shape[2] // gw

    @pl.when(pl.program_id(1) == 0)
    def _():
        s_ref[...] = jnp.zeros_like(s_ref)

    incl, strict, eye = _cat_tri_masks()
    bd_b = _block_mask(gw, gw, RWKV_HEAD_DIM, RWKV_HEAD_DIM)
    bd = jnp.where(bd_b, 1.0, 0.0).astype(BF16)

    chains = [(sq, gi) for sq in range(nseq) for gi in range(ngrp)]
    lanes = [slice(gi * gw, (gi + 1) * gw) for _, gi in chains]
    every = range(len(chains))

    def chunk(ci, carry):
        rows = pl.ds(pl.multiple_of(ci * c, c), c)
        at = lambda ref, n: ref[chains[n][0], rows, lanes[n]]
        v = [at(v_ref, n) for n in every]
        lhs = [jnp.concatenate([at(kap_ref, n), at(rt_ref, n)], axis=0) for n in every]
        ab = [_dot_nt(lhs[n], _bd_tile(at(bet_ref, n), bd)) for n in every]
        ak = [_dot_nt(lhs[n], _bd_tile(at(kt_ref, n), bd)) for n in every]
        p = [-jnp.where(strict, ab[n][0:c], 0.0) for n in every]
        t = [jnp.where(eye, 1.0, 0.0) + p[n] for n in every]
        for _ in range(int(math.log2(c)) - 1):
            pb = [x.astype(BF16) for x in p]
            p = [_dot(pb[n], _bd_tile(pb[n], bd)) for n in every]
            t = [t[n] + _dot(t[n].astype(BF16), _bd_tile(p[n].astype(BF16), bd)) for n in every]
        st = [s_ref[sq, gi] for sq, gi in chains]
        kr = [_dot_nt(lhs[n], st[n].astype(BF16)) for n in every]
        vbd = [_bd_tile(v[n], bd) for n in every]
        x = [kr[n][0:c] + _dot(jnp.where(strict, ak[n][0:c], 0.0).astype(BF16), vbd[n]) for n in every]
        ub = [_dot(t[n].astype(BF16), _bd_tile(x[n].astype(BF16), bd)).astype(BF16) for n in every]
        y = [kr[n][c:2 * c] + _dot(jnp.where(incl, ak[n][c:2 * c], 0.0).astype(BF16), vbd[n])
             - _dot(jnp.where(incl, ab[n][c:2 * c], 0.0).astype(BF16), _bd_tile(ub[n], bd)) for n in every]
        upd = [_dot_tn(jnp.concatenate([v[n], ub[n]], axis=0),
                       jnp.concatenate([at(ktc_ref, n), -at(betc_ref, n)], axis=0)) for n in every]
        mom = [_dot(jnp.concatenate([y[n], y[n] * y[n]], axis=0).astype(BF16), bd) * (1.0 / RWKV_HEAD_DIM)
               for n in every]
        for n, (sq, gi) in enumerate(chains):
            ec = ec_ref[sq, pl.ds(ci, 1), lanes[n]]
            s_ref[sq, gi] = st[n] * ec + jnp.where(bd_b, upd[n], 0.0)
            mean = mom[n][0:c]
            var = mom[n][c:2 * c] - mean * mean
            yn = (y[n] - mean) * lax.rsqrt(var + RWKV_LN_EPS) * lnw_ref[:, lanes[n]] + lnb_ref[:, lanes[n]]
            yn = (yn + at(bon_ref, n).astype(F32)) * at(g_ref, n).astype(F32)
            o_ref[sq, rows, lanes[n]] = yn.astype(o_ref.dtype)
        return carry

    lax.fori_loop(0, kap_ref.shape[1] // c, chunk, 0)


SEQ_PER_STEP = 8
REC_BLOCK = 128


def _seq_per_step(b):
    return max(n for n in range(1, SEQ_PER_STEP + 1) if b % n == 0)


def _per_chunk_spec(x, nsq):
    b, nc, w = x.shape
    cps = REC_BLOCK // CHUNK
    return x.reshape(b, nc // cps, cps, w), pl.BlockSpec((nsq, None, cps, w), lambda bi, i: (bi, i, 0, 0))


def _rwkv_chunk(ops, ec, ln_w, ln_b):
    b, s, w = ops[0].shape
    nsq = _seq_per_step(b)
    lblk = REC_BLOCK
    seq = pl.BlockSpec((nsq, lblk, w), lambda bi, i: (bi, i, 0))
    par = pl.BlockSpec((1, w), lambda bi, i: (0, 0))
    gw = HEADS_PER_PASS * RWKV_HEAD_DIM
    ec4, ec_spec = _per_chunk_spec(ec, nsq)
    return pl.pallas_call(
        _rwkv_chunk_kernel,
        grid=(b // nsq, s // lblk),
        in_specs=[seq] * 9 + [ec_spec, par, par],
        out_specs=seq,
        out_shape=jax.ShapeDtypeStruct((b, s, w), BF16),
        scratch_shapes=[pltpu.VMEM((nsq, w // gw, gw, gw), F32)],
        compiler_params=_cp("parallel", "arbitrary"),
        name="rwkv_chunk",
    )(*ops, ec4, ln_w.reshape(1, w), ln_b.reshape(1, w))


def _gdn_prep_kernel(x_ref, xp_ref, sm_ref, z_ref, cw_ref, alog_ref, dtb_ref, ltri_ref, utri_ref, same_ref, tot_ref,
                     q_ref, k_ref, kb_ref, vb_ref, kbe_ref, qg_ref, kg_ref, zs_ref, dec_ref, egl_ref):
    first = pl.program_id(1) == 0
    dh = GDN_HEAD_DIM
    w = MIX_W
    x = x_ref[...].astype(F32)
    ts = x.shape[0]
    prev8 = _prev_rows(xp_ref, first)
    acc = x * cw_ref[GDN_CONV - 1:GDN_CONV, :]
    for d in range(1, GDN_CONV):
        acc = acc + _shift_down(x, prev8, d) * cw_ref[GDN_CONV - 1 - d:GDN_CONV - d, :]
    act = acc * _sigmoid(acc)
    sm = sm_ref[...].astype(F32)
    gs = [-jnp.exp(alog_ref[:, h:h + 1]) * _softplus(sm[:, SM_GDN_ALPHA + h:SM_GDN_ALPHA + h + 1] + dtb_ref[:, h:h + 1])
          for h in range(GDN_HEADS)]
    gc = jnp.concatenate([jnp.broadcast_to(g, (ts, CHUNK)) for g in gs], axis=1)
    gc3 = _split3(gc)
    gamc = _sum01(ltri_ref[...], gc3)
    sufc = _sum01(utri_ref[...], gc3)
    t_in = lax.broadcasted_iota(jnp.int32, gc.shape, 0) % CHUNK
    s_in = lax.broadcasted_iota(jnp.int32, gc.shape, 1) % CHUNK
    gamr = _sum01(same_ref[...], jnp.where(t_in == s_in, gamc, 0.0))
    dec_ref[...] = jnp.exp(jnp.where(t_in >= s_in, gamc - gamr, NEG))
    eglc = jnp.exp(_sum01(tot_ref[...], gc3))[0:egl_ref.shape[0]]
    egl_ref[...] = jnp.concatenate([eglc[:, h * CHUNK:(h + 1) * CHUNK] for h in range(GDN_HEADS)
                                    for _ in range(dh // CHUNK)],
                                   axis=1)
    for h in range(GDN_HEADS):
        ls = slice(h * dh, (h + 1) * dh)
        q = act[:, h * dh:(h + 1) * dh]
        k = act[:, w + h * dh:w + (h + 1) * dh]
        v = act[:, 2 * w + h * dh:2 * w + (h + 1) * dh]
        q = q * lax.rsqrt(jnp.sum(q * q, axis=-1, keepdims=True) + EPS) * (dh ** -0.5)
        k = k * lax.rsqrt(jnp.sum(k * k, axis=-1, keepdims=True) + EPS)
        beta = _sigmoid(sm[:, SM_GDN_BETA + h:SM_GDN_BETA + h + 1])
        eg = jnp.exp(gamc[:, h * CHUNK:h * CHUNK + 1])
        es = jnp.exp(sufc[:, h * CHUNK:h * CHUNK + 1])
        kb = k * beta
        q_ref[:, ls] = q.astype(BF16)
        k_ref[:, ls] = k.astype(BF16)
        kb_ref[:, ls] = kb.astype(BF16)
        vb_ref[:, ls] = (v * beta).astype(BF16)
        kbe_ref[:, ls] = (kb * eg).astype(BF16)
        qg_ref[:, ls] = (q * eg).astype(BF16)
        kg_ref[:, ls] = (k * es).astype(BF16)
    z = z_ref[...].astype(F32)
    zs_ref[...] = (z * _sigmoid(z)).astype(BF16)


def _gdn_prep(proj3, conv_w, a_log, dt_bias):
    b, s, _ = proj3.shape
    w = MIX_W
    ts = PREP_TILE
    nrb = ts // PREV_ROWS
    nck = ts // CHUNK
    cw = HEADS_PER_PASS * CHUNK
    ltri, utri, same, tot = _chunk_consts(ts)
    full = lambda shp: pl.BlockSpec(shp, lambda bi, i: (0,) * len(shp))
    seq = pl.BlockSpec((None, ts, w), lambda bi, i: (bi, i, 0))
    return pl.pallas_call(
        _gdn_prep_kernel,
        grid=(b, s // ts),
        in_specs=[pl.BlockSpec((None, ts, 3 * w), lambda bi, i: (bi, i, C_GDN // (3 * w))),
                  pl.BlockSpec((None, PREV_ROWS, 3 * w),
                               lambda bi, i: (bi, jnp.maximum(i * nrb - 1, 0), C_GDN // (3 * w))),
                  pl.BlockSpec((None, ts, 128), lambda bi, i: (bi, i, C_SM0 // 128)),
                  pl.BlockSpec((None, ts, w), lambda bi, i: (bi, i, (C_GDN + 3 * w) // w)),
                  full((GDN_CONV, 3 * w)), full((1, GDN_HEADS)), full((1, GDN_HEADS)),
                  full((ts, ts)), full((ts, ts)), full((ts, ts)), full(tot.shape)],
        out_specs=[seq] * 8 + [pl.BlockSpec((None, ts, cw), lambda bi, i: (bi, i, 0)),
                               pl.BlockSpec((None, nck, w), lambda bi, i: (bi, i, 0))],
        out_shape=[jax.ShapeDtypeStruct((b, s, w), BF16)] * 8 + [jax.ShapeDtypeStruct((b, s, cw), F32),
                                                                 jax.ShapeDtypeStruct((b, s // CHUNK, w), F32)],
        compiler_params=_cp("parallel", "parallel"),
        name="gdn_prep",
    )(proj3, proj3, proj3, proj3, conv_w, a_log.reshape(1, GDN_HEADS), dt_bias.reshape(1, GDN_HEADS),
      ltri, utri, same, tot)


def _gdn_chunk_kernel(q_ref, k_ref, kb_ref, vb_ref, kbe_ref, qg_ref, kg_ref, zs_ref, dec_ref, egl_ref, nw_ref,
                      o_ref, s_ref):
    c = CHUNK
    dh = GDN_HEAD_DIM
    w = MIX_W
    pw = 2 * dh
    npair = w // pw
    nseq = q_ref.shape[0]

    @pl.when(pl.program_id(1) == 0)
    def _():
        s_ref[...] = jnp.zeros_like(s_ref)

    _, strict, eye = _cat_tri_masks()
    cw = HEADS_PER_PASS * c
    bd64 = jnp.where(_block_mask(cw, cw, c, c), 1.0, 0.0).astype(BF16)
    bdk = jnp.where(_block_mask(cw, w, c, dh), 1.0, 0.0).astype(BF16)
    bdp_b = _block_mask(pw, pw, dh, dh)

    seqs = range(nseq)
    pairs = [slice(p * pw, (p + 1) * pw) for p in range(npair)]

    def chunk(ci, carry):
        rows = pl.ds(pl.multiple_of(ci * c, c), c)
        lhs = [jnp.concatenate([kb_ref[sq, rows, :], q_ref[sq, rows, :]], axis=0) for sq in seqs]
        aq = [_dot_nt(lhs[sq], _bd_tile(k_ref[sq, rows, :], bdk)) for sq in seqs]
        dec = [dec_ref[sq, rows, :] for sq in seqs]
        p = [-jnp.where(strict, aq[sq][0:c] * dec[sq], 0.0) for sq in seqs]
        t = [jnp.where(eye, 1.0, 0.0) + p[sq] for sq in seqs]
        for _ in range(int(math.log2(c)) - 1):
            pb = [x.astype(BF16) for x in p]
            p = [_dot(pb[sq], _bd_tile(pb[sq], bd64)) for sq in seqs]
            t = [t[sq] + _dot(t[sq].astype(BF16), _bd_tile(p[sq].astype(BF16), bd64)) for sq in seqs]
        tb = [x.astype(BF16) for x in t]
        u = [_dot(tb[sq], _bd_tile(vb_ref[sq, rows, :], bdk)) for sq in seqs]
        wm = [_dot(tb[sq], _bd_tile(kbe_ref[sq, rows, :], bdk)).astype(BF16) for sq in seqs]
        st = [[s_ref[sq, pi] for pi in range(npair)] for sq in seqs]
        stb = [[x.astype(BF16) for x in st[sq]] for sq in seqs]
        ws = [jnp.concatenate([_dot(wm[sq][:, ps], stb[sq][pi]) for pi, ps in enumerate(pairs)], axis=1)
              for sq in seqs]
        vnb = [(u[sq] - ws[sq]).astype(BF16) for sq in seqs]
        qs = [jnp.concatenate([_dot(qg_ref[sq, rows, ps], stb[sq][pi]) for pi, ps in enumerate(pairs)], axis=1)
              for sq in seqs]
        o = [qs[sq] + _dot((aq[sq][c:2 * c] * dec[sq]).astype(BF16), _bd_tile(vnb[sq], bdk)) for sq in seqs]
        upd = [[_dot_tn(kg_ref[sq, rows, ps], vnb[sq][:, ps]) for ps in pairs] for sq in seqs]
        for sq in seqs:
            egl = egl_ref[sq, pl.ds(ci, 1), :]
            for pi, ps in enumerate(pairs):
                s_ref[sq, pi] = st[sq][pi] * egl[:, ps] + jnp.where(bdp_b, upd[sq][pi], 0.0)
            for h in range(GDN_HEADS):
                ls = slice(h * dh, (h + 1) * dh)
                oh = o[sq][:, ls]
                ms = jnp.mean(oh * oh, axis=-1, keepdims=True)
                o_ref[sq, rows, ls] = (oh * lax.rsqrt(ms + EPS) * nw_ref[...]
                                       * zs_ref[sq, rows, ls].astype(F32)).astype(o_ref.dtype)
        return carry

    lax.fori_loop(0, q_ref.shape[1] // c, chunk, 0)


def _gdn_chunk(ops, dec, egl, norm_w):
    b, s, w = ops[0].shape
    dh = GDN_HEAD_DIM
    nsq = _seq_per_step(b)
    lblk = REC_BLOCK
    seq = pl.BlockSpec((nsq, lblk, w), lambda bi, i: (bi, i, 0))
    egl, egl_spec = _per_chunk_spec(egl, nsq)
    return pl.pallas_call(
        _gdn_chunk_kernel,
        grid=(b // nsq, s // lblk),
        in_specs=[seq] * 8 + [pl.BlockSpec((nsq, lblk, dec.shape[2]), lambda bi, i: (bi, i, 0)), egl_spec,
                              pl.BlockSpec((1, dh), lambda bi, i: (0, 0))],
        out_specs=seq,
        out_shape=jax.ShapeDtypeStruct((b, s, w), BF16),
        scratch_shapes=[pltpu.VMEM((nsq, w // (2 * dh), 2 * dh, 2 * dh), F32)],
        compiler_params=_cp("parallel", "arbitrary"),
        name="gdn_chunk",
    )(*ops, dec, egl, norm_w.reshape(1, dh))


def _merge_kernel(x_ref, oa_ref, ob_ref, oc_ref, ga_ref, gb_ref, gc_ref, wb_ref, wo_ref, o_ref):
    merged = None
    for j, (br, gate) in enumerate(((oa_ref, ga_ref), (ob_ref, gb_ref), (oc_ref, gc_ref))):
        t = _sigmoid(gate[...].astype(F32)) * _dot(br[...], wb_ref[j])
        merged = t if merged is None else merged + t
    o_ref[...] = x_ref[...] + _dot(merged.astype(BF16), wo_ref[...])


def _merge(x2, o_a, o_b, o_c, proj, w_branch, w_out, tm=512):
    t, d = x2.shape
    w = MIX_W
    tm = min(tm, t)
    br = pl.BlockSpec((tm, w), lambda i: (i, 0))
    gate = lambda j: pl.BlockSpec((tm, d), lambda i: (i, C_GATE // d + j))
    return pl.pallas_call(
        _merge_kernel,
        grid=(t // tm,),
        in_specs=[pl.BlockSpec((tm, d), lambda i: (i, 0)), br, br, br, gate(0), gate(1), gate(2),
                  pl.BlockSpec((3, w, d), lambda i: (0, 0, 0)),
                  pl.BlockSpec((d, d), lambda i: (0, 0))],
        out_specs=pl.BlockSpec((tm, d), lambda i: (i, 0)),
        out_shape=jax.ShapeDtypeStruct((t, d), F32),
        compiler_params=_cp("parallel"),
        name="merge",
    )(x2, o_a.reshape(t, w), o_b.reshape(t, w), o_c.reshape(t, w), proj, proj, proj,
      w_branch.astype(BF16), w_out.astype(BF16))


def _ffn_kernel(x_ref, g_ref, w1_ref, w2_ref, o_ref, hn_ref):
    j = pl.program_id(1)

    @pl.when(j == 0)
    def _():
        x = x_ref[...]
        ms = jnp.mean(x * x, axis=-1, keepdims=True)
        hn_ref[...] = (x * lax.rsqrt(ms + EPS) * g_ref[...]).astype(BF16)
        o_ref[...] = x

    h1 = jnp.maximum(_dot(hn_ref[...], w1_ref[...]), 0.0)
    o_ref[...] += _dot((h1 * h1).astype(BF16), w2_ref[...])


def _ffn(x2, g, w1, w2, tm=1024, tf=1024):
    t, d = x2.shape
    f = w1.shape[1]
    tm = min(tm, t)
    return pl.pallas_call(
        _ffn_kernel,
        grid=(t // tm, f // tf),
        in_specs=[pl.BlockSpec((tm, d), lambda i, j: (i, 0)),
                  pl.BlockSpec((1, d), lambda i, j: (0, 0)),
                  pl.BlockSpec((d, tf), lambda i, j: (0, j)),
                  pl.BlockSpec((tf, d), lambda i, j: (j, 0))],
        out_specs=pl.BlockSpec((tm, d), lambda i, j: (i, 0)),
        out_shape=jax.ShapeDtypeStruct((t, d), F32),
        scratch_shapes=[pltpu.VMEM((tm, d), BF16)],
        compiler_params=_cp("parallel", "arbitrary"),
        name="ffn",
    )(x2, g.reshape(1, d), w1.astype(BF16), w2.astype(BF16))


def kernel(x, norm_mix_g, w_in, nsa_q_norm, nsa_k_norm, nsa_cmp_pos, nsa_cmp_w1, nsa_cmp_w2, rwkv_mu, rwkv_w0, rwkv_w_up, rwkv_a0, rwkv_a_up, rwkv_g_up, rwkv_k_k, rwkv_k_a, rwkv_r_k, rwkv_ln_w, rwkv_ln_b, rwkv_v0, rwkv_vres_down, rwkv_vres_up, gdn_conv_w, gdn_a_log, gdn_dt_bias, gdn_norm_w, w_branch, w_out, norm_ffn_g, w_ff1, w_ff2):
    b, s, d = x.shape
    depth = w_in.shape[0]
    perm = jnp.asarray(_proj_perm())
    x2 = x.reshape(b * s, d)
    v_first = None
    for i in range(depth):
        vd = rwkv_vres_down[i - 1] if i > 0 else jnp.zeros((d, VRES_RANK), F32)
        w_ext = jnp.concatenate([w_in[i], vd, jnp.zeros((d, 1), F32)], axis=1)
        w_all = jnp.take(w_ext, perm, axis=1).astype(BF16)
        proj = _rms_matmul(x2, norm_mix_g[i], w_all)
        proj3 = proj.reshape(b, s, N_PROJ)

        o_a = _nsa(proj3, nsa_q_norm[i], nsa_k_norm[i], nsa_cmp_pos[i], nsa_cmp_w1[i], nsa_cmp_w2[i])

        vres = None if i == 0 else (rwkv_v0[i - 1], rwkv_vres_up[i - 1])
        rw = _rwkv_prep(proj3, rwkv_mu[i], rwkv_w0[i], rwkv_w_up[i], rwkv_a0[i], rwkv_a_up[i], rwkv_g_up[i],
                        rwkv_k_k[i], rwkv_k_a[i], rwkv_r_k[i], v_first, vres)
        if i == 0:
            v_first = rw[10]
        o_b = _rwkv_chunk(rw[:9], rw[9], rwkv_ln_w[i], rwkv_ln_b[i])

        gd = _gdn_prep(proj3, gdn_conv_w[i], gdn_a_log[i], gdn_dt_bias[i])
        o_c = _gdn_chunk(gd[:8], gd[8], gd[9], gdn_norm_w[i])

        x2 = _merge(x2, o_a, o_b, o_c, proj, w_branch[i], w_out[i])
        x2 = _ffn(x2, norm_ffn_g[i], w_ff1[i], w_ff2[i])
    return x2.reshape(b, s, d)
```

```python
import functools
import math

import numpy as np
import jax
import jax.numpy as jnp
from jax import lax
from jax.experimental import pallas as pl
from jax.experimental.pallas import tpu as pltpu

F32 = jnp.float32
BF16 = jnp.bfloat16

D_MODEL = 1024
MIX_W = 512
NSA_HEAD_DIM = 64
NSA_KV_GROUPS = 2
NSA_HPG = 4
NSA_CMP_BLOCK = 32
NSA_CMP_STRIDE = 16
NSA_CMP_HIDDEN = 256
NSA_SEL_BLOCK = 64
NSA_N_SEL = 16
NSA_WINDOW = 512
Q_BLOCK = 128
NSA_IN = 1304
RWKV_HEADS = 8
RWKV_HEAD_DIM = 64
RWKV_LN_EPS = 64e-5
RWKV_IN = 1792
GDN_HEADS = 4
GDN_HEAD_DIM = 128
GDN_CONV = 4
GDN_IN = 2056
D_IN = 8224
D_FF = 4096
EPS = 1e-6
NEG = -1e30
FORCE = 1e4

C_KV = 0
C_Q = 768
C_SM0 = 1280
C_SM1 = 1408
SM_GDN_ALPHA = 16
SM_GDN_BETA = 20
SM_VRES = 32
VRES_RANK = 32
C_RKV = 1536
C_GDN = 3072
C_GATE = 5120
C_LR = 8192
N_PROJ = 8448
VMEM_LIMIT = 48 * 1024 * 1024


def _proj_perm():
    zero = D_IN + VRES_RANK
    perm = np.full((N_PROJ,), zero, np.int32)
    perm[C_KV:C_KV + 768] = np.arange(512, 1280)
    perm[C_Q:C_Q + 512] = np.arange(0, 512)
    perm[C_SM0:C_SM0 + 12] = 1280 + np.arange(12)
    perm[C_SM1:C_SM1 + 12] = 1292 + np.arange(12)
    rw = NSA_IN
    gd = NSA_IN + RWKV_IN
    gt = gd + GDN_IN
    perm[C_SM0 + SM_GDN_ALPHA:C_SM0 + SM_GDN_ALPHA + GDN_HEADS] = gd + 2048 + np.arange(GDN_HEADS)
    perm[C_SM0 + SM_GDN_BETA:C_SM0 + SM_GDN_BETA + GDN_HEADS] = gd + 2048 + GDN_HEADS + np.arange(GDN_HEADS)
    perm[C_SM0 + SM_VRES:C_SM0 + SM_VRES + VRES_RANK] = D_IN + np.arange(VRES_RANK)
    perm[C_RKV:C_RKV + 1536] = rw + np.arange(1536)
    perm[C_LR:C_LR + 256] = rw + 1536 + np.arange(256)
    perm[C_GDN:C_GDN + 2048] = gd + np.arange(2048)
    perm[C_GATE:C_GATE + 3072] = gt + np.arange(3072)
    return perm


def _cp(*sem):
    return pltpu.CompilerParams(dimension_semantics=sem, vmem_limit_bytes=VMEM_LIMIT)


def _dot(a, b):
    return jnp.dot(a, b, preferred_element_type=F32)


def _dot_nt(a, b):
    return lax.dot_general(a, b, (((1,), (1,)), ((), ())), preferred_element_type=F32)


def _dot_tn(a, b):
    return lax.dot_general(a, b, (((0,), (0,)), ((), ())), preferred_element_type=F32)


def _split(a):
    hi = a.astype(BF16)
    lo = (a - hi.astype(F32)).astype(BF16)
    return hi, lo


def _mm3(a, b):
    ah, al = _split(a)
    bh, bl = _split(b)
    return _dot(ah, bh) + (_dot(ah, bl) + _dot(al, bh))


def _split3(x):
    hi = x.astype(BF16)
    r1 = x - hi.astype(F32)
    mid = r1.astype(BF16)
    return hi, mid, (r1 - mid.astype(F32)).astype(BF16)


def _sum01(m, x):
    hi, mid, lo = x if isinstance(x, tuple) else _split3(x)
    return _dot(m, hi) + (_dot(m, mid) + _dot(m, lo))


def _segsum_left(m, x):
    hi, lo = _split(x)
    return _dot(m, hi) + _dot(m, lo)


def _segsum(x, seg):
    hi, lo = _split(x)
    return _dot(hi, seg) + _dot(lo, seg)


CHUNK = 64
HEADS_PER_PASS = 4


def _chunk_consts(ts):
    r = np.arange(ts)
    same = (r[:, None] // CHUNK) == (r[None, :] // CHUNK)
    ltri = same & (r[:, None] >= r[None, :])
    utri = same & (r[:, None] < r[None, :])
    nck = ts // CHUNK
    tot = np.zeros((max(nck, 8), ts), bool)
    tot[:nck] = (r[None, :] // CHUNK) == np.arange(nck)[:, None]
    return [jnp.asarray(m, BF16) for m in (ltri, utri, same, tot)]


def _block_mask(rows, cols, rblk, cblk):
    r = lax.broadcasted_iota(jnp.int32, (rows, cols), 0) // rblk
    c = lax.broadcasted_iota(jnp.int32, (rows, cols), 1) // cblk
    return r == c


def _cat_tri_masks():
    w = HEADS_PER_PASS * CHUNK
    t = lax.broadcasted_iota(jnp.int32, (CHUNK, w), 0)
    s = lax.broadcasted_iota(jnp.int32, (CHUNK, w), 1) % CHUNK
    return t >= s, t > s, t == s


def _bd_tile(y, bd):
    return jnp.concatenate([y] * HEADS_PER_PASS, axis=0) * bd


def _softplus(z):
    return jnp.maximum(z, 0.0) + jnp.log1p(jnp.exp(-jnp.abs(z)))


def _sigmoid(z):
    return 1.0 / (1.0 + jnp.exp(-z))


def _rms_matmul_kernel(x_ref, g_ref, w_ref, o_ref, hn_ref):
    @pl.when(pl.program_id(1) == 0)
    def _():
        x = x_ref[...]
        ms = jnp.mean(x * x, axis=-1, keepdims=True)
        hn_ref[...] = (x * lax.rsqrt(ms + EPS) * g_ref[...]).astype(BF16)

    o_ref[...] = _dot(hn_ref[...], w_ref[...]).astype(o_ref.dtype)


def _rms_matmul(x2, g, w, tm=2048, tn=768):
    t, d = x2.shape
    n = w.shape[1]
    tm = min(tm, t)
    return pl.pallas_call(
        _rms_matmul_kernel,
        grid=(t // tm, n // tn),
        in_specs=[pl.BlockSpec((tm, d), lambda i, j: (i, 0)),
                  pl.BlockSpec((1, d), lambda i, j: (0, 0)),
                  pl.BlockSpec((d, tn), lambda i, j: (0, j))],
        out_specs=pl.BlockSpec((tm, tn), lambda i, j: (i, j)),
        out_shape=jax.ShapeDtypeStruct((t, n), BF16),
        scratch_shapes=[pltpu.VMEM((tm, d), BF16)],
        compiler_params=_cp("parallel", "arbitrary"),
        name="rms_proj",
    )(x2, g.reshape(1, d), w)


def _nsa_kvprep_kernel(kv_ref, kg_ref, kc_ref, vc_ref, ks_ref, vs_ref, kw_ref, vw_ref):
    outs = (kc_ref, vc_ref, ks_ref, vs_ref, kw_ref, vw_ref)
    for j in range(6):
        for g in range(NSA_KV_GROUPS):
            lo = j * 128 + g * NSA_HEAD_DIM
            piece = kv_ref[:, lo:lo + NSA_HEAD_DIM].astype(F32)
            if j in (2, 4):
                gain = kg_ref[j // 2:j // 2 + 1, :]
                ms = jnp.mean(piece * piece, axis=-1, keepdims=True)
                piece = piece * lax.rsqrt(ms + EPS) * gain
            outs[j][g] = piece.astype(outs[j].dtype)


def _nsa_kvprep(proj3, k_g, ts=1024):
    b, s, _ = proj3.shape
    g, dh = NSA_KV_GROUPS, NSA_HEAD_DIM
    out_spec = lambda wd: pl.BlockSpec((None, g, ts, wd), lambda bi, i: (bi, 0, i, 0))
    shp = lambda dt, wd=dh: jax.ShapeDtypeStruct((b, g, s, wd), dt)
    return pl.pallas_call(
        _nsa_kvprep_kernel,
        grid=(b, s // ts),
        in_specs=[pl.BlockSpec((None, ts, 768), lambda bi, i: (bi, i, C_KV // 768)),
                  pl.BlockSpec((3, dh), lambda bi, i: (0, 0))],
        out_specs=[out_spec(dh)] * 6,
        out_shape=[shp(F32), shp(F32), shp(BF16), shp(BF16), shp(BF16), shp(BF16)],
        compiler_params=_cp("parallel", "parallel"),
        name="nsa_kvprep",
    )(proj3, k_g)


def _nsa_compress_kernel(x_ref, w1_ref, pos_ref, w2_ref, g_ref, o_ref, *, normalize):
    x = x_ref[...].astype(BF16)
    half = x.shape[1]
    a = _dot(x, w1_ref[0:half, :])
    bm = _dot(x, w1_ref[half:2 * half, :])
    bias = _dot(pos_ref[...].astype(BF16), w1_ref[...])[0:1, :]
    n = x.shape[0]
    hid = a + pltpu.roll(bm, n - 1, 0) + bias
    hid = hid * _sigmoid(hid)
    out = _dot(hid.astype(BF16), w2_ref[...])
    if normalize:
        ms = jnp.mean(out * out, axis=-1, keepdims=True)
        out = out * lax.rsqrt(ms + EPS) * g_ref[...]
    o_ref[...] = out.astype(o_ref.dtype)


def _nsa_compress(xh, w1, pos, w2, gain, normalize):
    bg, n, half = xh.shape
    dh = NSA_HEAD_DIM
    pos8 = jnp.broadcast_to(pos.reshape(1, 2 * half), (8, 2 * half))
    return pl.pallas_call(
        functools.partial(_nsa_compress_kernel, normalize=normalize),
        grid=(bg,),
        in_specs=[pl.BlockSpec((None, n, half), lambda i: (i, 0, 0)),
                  pl.BlockSpec((2 * half, NSA_CMP_HIDDEN), lambda i: (0, 0)),
                  pl.BlockSpec((8, 2 * half), lambda i: (0, 0)),
                  pl.BlockSpec((NSA_CMP_HIDDEN, dh), lambda i: (0, 0)),
                  pl.BlockSpec((1, dh), lambda i: (0, 0))],
        out_specs=pl.BlockSpec((None, n, dh), lambda i: (i, 0, 0)),
        out_shape=jax.ShapeDtypeStruct((bg, n, dh), BF16),
        compiler_params=_cp("parallel"),
        name="nsa_compress",
    )(xh, w1.astype(BF16), pos8, w2.astype(BF16), gain.reshape(1, dh))


SEL_TILE = 256
STRIP = 128
LOG2E = 1.4426950408889634


def _softmax_cols(s, valid):
    s = jnp.where(valid, s, NEG)
    p = jnp.exp2(s - jnp.max(s, axis=0, keepdims=True))
    return p, jnp.sum(p, axis=0, keepdims=True)


def _nsa_attn_t_kernel(q_ref, sm_ref, kc_ref, vct_ref, ks_ref, vst_ref, kw_ref, vwt_ref, c2s_ref, qg_ref, o_ref,
                       sb_ref, acc_ref, al_ref, m_ref, l_ref, sa_ref, sb2_ref, pa_ref, pb_ref, *, n_sel):
    dh, hg, qb = NSA_HEAD_DIM, NSA_HPG, Q_BLOCK
    cols = hg * qb
    tk = SEL_TILE
    blk = pl.program_id(2)
    start = blk * qb

    xt = jnp.transpose(q_ref[...].astype(F32))
    qs = []
    for h in range(hg):
        xh = xt[h * dh:(h + 1) * dh, :]
        ms = jnp.mean(xh * xh, axis=0, keepdims=True)
        qs.append(xh * lax.rsqrt(ms + EPS) * qg_ref[...])
    qt = jnp.concatenate(qs, axis=1).astype(BF16)
    tq_l = start + lax.broadcasted_iota(jnp.int32, (1, qb), 1)
    heads = [slice(h * qb, (h + 1) * qb) for h in range(hg)]

    t_col = jnp.concatenate([tq_l] * hg, axis=1)

    n_cmp = kc_ref.shape[0]
    cvalid = (lax.broadcasted_iota(jnp.int32, (n_cmp, 1), 0) * NSA_CMP_STRIDE + (NSA_CMP_BLOCK - 1)) <= t_col
    p, l = _softmax_cols(_dot(kc_ref[...], qt), cvalid)
    p = p * jnp.where(t_col >= NSA_CMP_BLOCK - 1, 1.0 / l, 0.0)
    o_c = _dot(vct_ref[...], p.astype(BF16))
    psum = p[:, heads[0]]
    for cs in heads[1:]:
        psum = psum + p[:, cs]
    imp_t = _segsum_left(c2s_ref[...], psum)

    n_slc = imp_t.shape[0]
    jr = lax.broadcasted_iota(jnp.int32, (n_slc, 1), 0)
    cur = jnp.right_shift(tq_l, int(math.log2(NSA_SEL_BLOCK)))
    forced = (jr == 0) | (jr == cur) | (jr == cur - 1)
    causal = jr * NSA_SEL_BLOCK <= tq_l
    val = jnp.where(forced, -jnp.inf, jnp.where(causal, imp_t, -FORCE))
    jrf = jr.astype(F32)

    wlen = NSA_WINDOW + qb
    base = pl.multiple_of(jnp.maximum(start - NSA_WINDOW, 0), qb)
    dist = t_col - (base + lax.broadcasted_iota(jnp.int32, (wlen, 1), 0))
    in_window = lax.bitcast_convert_type(dist, jnp.uint32) < jnp.uint32(NSA_WINDOW)
    p, l = _softmax_cols(_dot(kw_ref[pl.ds(base, wlen), :], qt), in_window)
    o_w = _dot(vwt_ref[:, pl.ds(base, wlen)], p.astype(BF16)) * (1.0 / l)

    for _ in range(n_sel - 3):
        mx = jnp.max(val, axis=0, keepdims=True)
        idx = jnp.min(jnp.where(val == mx, jrf, float(n_slc)), axis=0, keepdims=True)
        val = jnp.where(jrf == idx, -jnp.inf, val)
    sb_ref[...] = jnp.where(val == -jnp.inf, 0.0, NEG)

    bpt = tk // NSA_SEL_BLOCK
    last = start // tk
    n_pairs = (last + 1) // 2
    strips = [(slice(c, c + STRIP), c % qb) for c in range(0, cols, STRIP)]

    def key_off(i):
        return pl.multiple_of(jnp.minimum(i, last) * tk, tk)

    def scores(i):
        return _dot(ks_ref[pl.ds(key_off(i), tk), :], qt)

    def step(tile, prev_tile, next_tile, s_cur, s_nxt, p_cur, p_prev, first):
        s_nxt[...] = scores(next_tile)
        if not first:
            pv = _dot(vst_ref[:, pl.ds(key_off(prev_tile), tk)], p_prev[...])
        rows = [sb_ref[pl.ds(jnp.minimum(tile, last) * bpt + j, 1), :] for j in range(bpt)]
        if first:
            kpos = tile * tk + lax.broadcasted_iota(jnp.int32, (tk, 1), 0)
        else:
            rows = [jnp.where(tile < last, r, NEG) for r in rows]
        for cs, off in strips:
            ts_ = slice(off, off + STRIP)
            sh = jnp.concatenate([s_cur[j * NSA_SEL_BLOCK:(j + 1) * NSA_SEL_BLOCK, cs] + rows[j][:, ts_]
                                  for j in range(bpt)], axis=0)
            if first:
                sh = jnp.where(kpos <= tq_l[:, ts_], sh, NEG)
            m_old = m_ref[:, cs]
            m_new = jnp.maximum(m_old, jnp.max(sh, axis=0, keepdims=True))
            alpha = jnp.exp2(m_old - m_new)
            p = jnp.exp2(sh - m_new)
            p_cur[:, cs] = p.astype(BF16)
            m_ref[:, cs] = m_new
            l_ref[:, cs] = alpha * l_ref[:, cs] + jnp.sum(p, axis=0, keepdims=True)
            if not first:
                acc_ref[:, cs] = al_ref[:, cs] * acc_ref[:, cs] + pv[:, cs]
            al_ref[:, cs] = alpha

    m_ref[...] = jnp.full((1, cols), NEG, F32)
    l_ref[...] = jnp.zeros((1, cols), F32)
    acc_ref[...] = jnp.zeros((dh, cols), F32)
    sa_ref[...] = scores(last)
    step(last, last, 0, sa_ref, sb2_ref, pa_ref, pb_ref, True)

    def sel_body(j, carry):
        t0 = 2 * j
        step(t0, jnp.where(j == 0, last, t0 - 1), t0 + 1, sb2_ref, sa_ref, pb_ref, pa_ref, False)
        step(t0 + 1, t0, t0 + 2, sa_ref, sb2_ref, pa_ref, pb_ref, False)
        return carry

    lax.fori_loop(0, n_pairs, sel_body, 0)
    prev = jnp.where(n_pairs == 0, last, 2 * n_pairs - 1)
    acc = al_ref[...] * acc_ref[...] + _dot(vst_ref[:, pl.ds(key_off(prev), tk)], pa_ref[...])
    o_s = acc * (1.0 / l_ref[...])

    gt = jnp.transpose(_sigmoid(sm_ref[...].astype(F32)))
    outs = []
    for h, cs in enumerate(heads):
        outs.append(gt[3 * h:3 * h + 1, :] * o_c[:, cs] + gt[3 * h + 1:3 * h + 2, :] * o_s[:, cs]
                    + gt[3 * h + 2:3 * h + 3, :] * o_w[:, cs])
    o_ref[...] = jnp.transpose(jnp.concatenate(outs, axis=0)).astype(o_ref.dtype)


def _nsa(proj3, q_g, k_g, cmp_pos, cmp_w1, cmp_w2):
    b, s, _ = proj3.shape
    g, dh, qb = NSA_KV_GROUPS, NSA_HEAD_DIM, Q_BLOCK
    kc_raw, vc_raw, ks, vs, kw, vw = _nsa_kvprep(proj3, k_g, ts=min(1024, s))
    n_half = s // NSA_CMP_STRIDE
    half = NSA_CMP_STRIDE * dh
    kc = _nsa_compress(kc_raw.reshape(b * g, n_half, half), cmp_w1[0], cmp_pos[0], cmp_w2[0], k_g[0], True)
    vc = _nsa_compress(vc_raw.reshape(b * g, n_half, half), cmp_w1[1], cmp_pos[1], cmp_w2[1], k_g[0], False)
    kc = kc.reshape(b, g, n_half, dh)
    vc = vc.reshape(b, g, n_half, dh)

    n_slc = s // NSA_SEL_BLOCK
    n_sel = min(NSA_N_SEL, n_slc)
    cmp_start = np.arange(n_half) * NSA_CMP_STRIDE
    slc_start = np.arange(n_slc) * NSA_SEL_BLOCK
    overlap = np.clip(np.minimum(cmp_start[:, None] + NSA_CMP_BLOCK, slc_start[None, :] + NSA_SEL_BLOCK)
                      - np.maximum(cmp_start[:, None], slc_start[None, :]), 0, None)
    c2s = (overlap / NSA_CMP_BLOCK).T

    k_spec = lambda n, wd=dh: pl.BlockSpec((None, None, n, wd), lambda bi, gi, i: (bi, gi, 0, 0))
    vt_spec = lambda n: pl.BlockSpec((None, None, dh, n), lambda bi, gi, i: (bi, gi, 0, 0))
    tr = lambda v: jnp.swapaxes(v, 2, 3)
    cols = NSA_HPG * qb
    q_gain = jnp.broadcast_to((q_g * (dh ** -0.5 * LOG2E)).reshape(dh, 1), (dh, qb))
    return pl.pallas_call(
        functools.partial(_nsa_attn_t_kernel, n_sel=n_sel),
        grid=(b, g, s // qb),
        in_specs=[pl.BlockSpec((None, qb, 256), lambda bi, gi, i: (bi, i, C_Q // 256 + gi)),
                  pl.BlockSpec((None, qb, 128), lambda bi, gi, i: (bi, i, C_SM0 // 128 + gi)),
                  k_spec(n_half), vt_spec(n_half), k_spec(s), vt_spec(s), k_spec(s), vt_spec(s),
                  pl.BlockSpec((n_slc, n_half), lambda bi, gi, i: (0, 0)),
                  pl.BlockSpec((dh, qb), lambda bi, gi, i: (0, 0))],
        out_specs=pl.BlockSpec((None, qb, 256), lambda bi, gi, i: (bi, i, gi)),
        out_shape=jax.ShapeDtypeStruct((b, s, MIX_W), BF16),
        scratch_shapes=([pltpu.VMEM((n_slc, qb), F32), pltpu.VMEM((dh, cols), F32),
                         pltpu.VMEM((1, cols), F32), pltpu.VMEM((1, cols), F32), pltpu.VMEM((1, cols), F32)]
                        + [pltpu.VMEM((SEL_TILE, cols), F32)] * 2 + [pltpu.VMEM((SEL_TILE, cols), BF16)] * 2),
        compiler_params=_cp("parallel", "parallel", "arbitrary"),
        name="nsa_attn",
    )(proj3, proj3, kc, tr(vc), ks, tr(vs), kw, tr(vw), jnp.asarray(c2s, BF16), q_gain)


def _prev_rows(prev_ref, first):
    return jnp.where(first, 0.0, prev_ref[PREV_ROWS - 8:PREV_ROWS, :].astype(F32))


def _shift_down(x, prev8, d):
    rolled = pltpu.roll(x, d, 0)
    row8 = lax.broadcasted_iota(jnp.int32, prev8.shape, 0)
    top = jnp.where(row8 < d, pltpu.roll(prev8, d, 0), rolled[0:8])
    return jnp.concatenate([top, rolled[8:]], axis=0)


PREP_TILE = 512
PREV_ROWS = 16


def _rwkv_prep_kernel(*refs, has_vres):
    (rkv_ref, rkvp_ref, lr_ref, lrp_ref, mu1_ref, mu2_ref, w0_ref, wup_ref, a0_ref, aup_ref, gup_ref,
     kkp_ref, kap_ref, rkp_ref, seg_ref, ltri_ref, utri_ref, tot_ref) = refs[:18]
    if has_vres:
        sm_ref, vf_ref, v0_ref, vu_ref = refs[18:22]
        outs = refs[22:]
    else:
        outs = refs[18:]
    kapo_ref, beto_ref, kto_ref, rto_ref, ktc_ref, betc_ref, vo_ref, bon_ref, g_ref, ec_ref = outs[:10]
    first = pl.program_id(1) == 0
    w = MIX_W
    c = rkv_ref[...].astype(F32)
    c = c + (_shift_down(c, _prev_rows(rkvp_ref, first), 1) - c) * mu1_ref[...]
    lr = lr_ref[...].astype(F32)
    lr = lr + (_shift_down(lr, _prev_rows(lrp_ref, first), 1) - lr) * mu2_ref[...]
    r, k, v = c[:, 0:w], c[:, w:2 * w], c[:, 2 * w:3 * w]
    wd, ad, gd = lr[:, 0:64], lr[:, 64:128], lr[:, 128:256]
    wlog = -_softplus(-(w0_ref[...] + _mm3(jnp.tanh(wd), wup_ref[...]))) - 0.5
    lw = -jnp.exp(wlog)
    a = _sigmoid(a0_ref[...] + _mm3(ad, aup_ref[...]))
    g_ref[...] = _mm3(_sigmoid(gd), gup_ref[...]).astype(g_ref.dtype)
    if has_vres:
        mix = _sigmoid(v0_ref[...] + _mm3(sm_ref[:, SM_VRES:SM_VRES + VRES_RANK].astype(F32), vu_ref[...]))
        v = v + (vf_ref[...] - v) * mix
    else:
        outs[10][...] = v
    seg = seg_ref[...]
    kk = k * kkp_ref[...]
    kk = kk * lax.rsqrt(_segsum(kk * kk, seg) + EPS)
    k = k * (1.0 + (a - 1.0) * kap_ref[...])
    kka = kk * a
    lw3 = _split3(lw)
    lcum = _sum01(ltri_ref[...], lw3)
    e_suf = jnp.exp(_sum01(utri_ref[...], lw3))
    e_inv = jnp.exp(-lcum)
    kapo_ref[...] = (kk * jnp.exp(lcum - lw)).astype(BF16)
    beto_ref[...] = (kka * e_inv).astype(BF16)
    kto_ref[...] = (k * e_inv).astype(BF16)
    rto_ref[...] = (r * jnp.exp(lcum)).astype(BF16)
    ktc_ref[...] = (k * e_suf).astype(BF16)
    betc_ref[...] = (kka * e_suf).astype(BF16)
    vo_ref[...] = v.astype(BF16)
    bon_ref[...] = (_segsum(r * k * rkp_ref[...], seg) * v).astype(BF16)
    ec_ref[...] = jnp.exp(_sum01(tot_ref[...], lw3))[0:ec_ref.shape[0]]


def _rwkv_prep(proj3, mu, w0, w_up, a0, a_up, g_up, k_k, k_a, r_k, v_first, vres):
    b, s, _ = proj3.shape
    w = MIX_W
    has_vres = vres is not None
    ts = PREP_TILE
    nrb = ts // PREV_ROWS
    nck = ts // CHUNK
    cur = lambda wd, cb: pl.BlockSpec((None, ts, wd), lambda bi, i: (bi, i, cb))
    prev = lambda wd, cb: pl.BlockSpec((None, PREV_ROWS, wd), lambda bi, i: (bi, jnp.maximum(i * nrb - 1, 0), cb))
    full = lambda shp: pl.BlockSpec(shp, lambda bi, i: (0,) * len(shp))
    ltri, utri, _, tot = _chunk_consts(ts)
    hd = np.arange(w) // RWKV_HEAD_DIM
    seg = jnp.asarray(hd[:, None] == hd[None, :], BF16)
    in_specs = [cur(1536, C_RKV // 1536), prev(1536, C_RKV // 1536), cur(256, C_LR // 256), prev(256, C_LR // 256),
                full((1, 1536)), full((1, 256)), full((1, w)), full((64, w)), full((1, w)), full((64, w)),
                full((128, w)), full((1, w)), full((1, w)), full((1, w)), full((w, w)), full((ts, ts)),
                full((ts, ts)), full(tot.shape)]
    args = [proj3, proj3, proj3, proj3, mu[:1536].reshape(1, 1536), mu[1536:].reshape(1, 256), w0.reshape(1, w),
            w_up, a0.reshape(1, w), a_up, g_up, k_k.reshape(1, w), k_a.reshape(1, w), r_k.reshape(1, w),
            seg, ltri, utri, tot]
    if has_vres:
        v0, vu = vres
        in_specs += [cur(128, C_SM0 // 128), pl.BlockSpec((None, ts, w), lambda bi, i: (bi, i, 0)),
                     full((1, w)), full((VRES_RANK, w))]
        args += [proj3, v_first, v0.reshape(1, w), vu]
    seq = pl.BlockSpec((None, ts, w), lambda bi, i: (bi, i, 0))
    out_specs = [seq] * 9 + [pl.BlockSpec((None, nck, w), lambda bi, i: (bi, i, 0))]
    out_shape = [jax.ShapeDtypeStruct((b, s, w), BF16)] * 9 + [jax.ShapeDtypeStruct((b, s // CHUNK, w), F32)]
    if not has_vres:
        out_specs.append(seq)
        out_shape.append(jax.ShapeDtypeStruct((b, s, w), F32))
    return pl.pallas_call(
        functools.partial(_rwkv_prep_kernel, has_vres=has_vres),
        grid=(b, s // ts),
        in_specs=in_specs,
        out_specs=out_specs,
        out_shape=out_shape,
        compiler_params=_cp("parallel", "parallel"),
        name="rwkv_prep",
    )(*args)


def _rwkv_chunk_kernel(kap_ref, bet_ref, kt_ref, rt_ref, ktc_ref, betc_ref, v_ref, bon_ref, g_ref, ec_ref,
                       lnw_ref, lnb_ref, o_ref, s_ref):
    c = CHUNK
    gw = HEADS_PER_PASS * RWKV_HEAD_DIM
    nseq = kap_ref.shape[0]
    ngrp = kap_ref.shape[2] // gw

    @pl.when(pl.program_id(1) == 0)
    def _():
        s_ref[...] = jnp.zeros_like(s_ref)

    incl, strict, eye = _cat_tri_masks()
    bd_b = _block_mask(gw, gw, RWKV_HEAD_DIM, RWKV_HEAD_DIM)
    bd = jnp.where(bd_b, 1.0, 0.0).astype(BF16)

    chains = [(sq, gi) for sq in range(nseq) for gi in range(ngrp)]
    lanes = [slice(gi * gw, (gi + 1) * gw) for _, gi in chains]
    every = range(len(chains))

    def chunk(ci, carry):
        rows = pl.ds(pl.multiple_of(ci * c, c), c)
        at = lambda ref, n: ref[chains[n][0], rows, lanes[n]]
        v = [at(v_ref, n) for n in every]
        lhs = [jnp.concatenate([at(kap_ref, n), at(rt_ref, n)], axis=0) for n in every]
        ab = [_dot_nt(lhs[n], _bd_tile(at(bet_ref, n), bd)) for n in every]
        ak = [_dot_nt(lhs[n], _bd_tile(at(kt_ref, n), bd)) for n in every]
        p = [-jnp.where(strict, ab[n][0:c], 0.0) for n in every]
        t = [jnp.where(eye, 1.0, 0.0) + p[n] for n in every]
        for _ in range(int(math.log2(c)) - 1):
            pb = [x.astype(BF16) for x in p]
            p = [_dot(pb[n], _bd_tile(pb[n], bd)) for n in every]
            t = [t[n] + _dot(t[n].astype(BF16), _bd_tile(p[n].astype(BF16), bd)) for n in every]
        st = [s_ref[sq, gi] for sq, gi in chains]
        kr = [_dot_nt(lhs[n], st[n].astype(BF16)) for n in every]
        vbd = [_bd_tile(v[n], bd) for n in every]
        x = [kr[n][0:c] + _dot(jnp.where(strict, ak[n][0:c], 0.0).astype(BF16), vbd[n]) for n in every]
        ub = [_dot(t[n].astype(BF16), _bd_tile(x[n].astype(BF16), bd)).astype(BF16) for n in every]
        y = [kr[n][c:2 * c] + _dot(jnp.where(incl, ak[n][c:2 * c], 0.0).astype(BF16), vbd[n])
             - _dot(jnp.where(incl, ab[n][c:2 * c], 0.0).astype(BF16), _bd_tile(ub[n], bd)) for n in every]
        upd = [_dot_tn(jnp.concatenate([v[n], ub[n]], axis=0),
                       jnp.concatenate([at(ktc_ref, n), -at(betc_ref, n)], axis=0)) for n in every]
        mom = [_dot(jnp.concatenate([y[n], y[n] * y[n]], axis=0).astype(BF16), bd) * (1.0 / RWKV_HEAD_DIM)
               for n in every]
        for n, (sq, gi) in enumerate(chains):
            ec = ec_ref[sq, pl.ds(ci, 1), lanes[n]]
            s_ref[sq, gi] = st[n] * ec + jnp.where(bd_b, upd[n], 0.0)
            mean = mom[n][0:c]
            var = mom[n][c:2 * c] - mean * mean
            yn = (y[n] - mean) * lax.rsqrt(var + RWKV_LN_EPS) * lnw_ref[:, lanes[n]] + lnb_ref[:, lanes[n]]
            yn = (yn + at(bon_ref, n).astype(F32)) * at(g_ref, n).astype(F32)
            o_ref[sq, rows, lanes[n]] = yn.astype(o_ref.dtype)
        return carry

    lax.fori_loop(0, kap_ref.shape[1] // c, chunk, 0)


SEQ_PER_STEP = 8
REC_BLOCK = 128


def _seq_per_step(b):
    return max(n for n in range(1, SEQ_PER_STEP + 1) if b % n == 0)


def _per_chunk_spec(x, nsq):
    b, nc, w = x.shape
    cps = REC_BLOCK // CHUNK
    return x.reshape(b, nc // cps, cps, w), pl.BlockSpec((nsq, None, cps, w), lambda bi, i: (bi, i, 0, 0))


def _rwkv_chunk(ops, ec, ln_w, ln_b):
    b, s, w = ops[0].shape
    nsq = _seq_per_step(b)
    lblk = REC_BLOCK
    seq = pl.BlockSpec((nsq, lblk, w), lambda bi, i: (bi, i, 0))
    par = pl.BlockSpec((1, w), lambda bi, i: (0, 0))
    gw = HEADS_PER_PASS * RWKV_HEAD_DIM
    ec4, ec_spec = _per_chunk_spec(ec, nsq)
    return pl.pallas_call(
        _rwkv_chunk_kernel,
        grid=(b // nsq, s // lblk),
        in_specs=[seq] * 9 + [ec_spec, par, par],
        out_specs=seq,
        out_shape=jax.ShapeDtypeStruct((b, s, w), BF16),
        scratch_shapes=[pltpu.VMEM((nsq, w // gw, gw, gw), F32)],
        compiler_params=_cp("parallel", "arbitrary"),
        name="rwkv_chunk",
    )(*ops, ec4, ln_w.reshape(1, w), ln_b.reshape(1, w))


def _gdn_prep_kernel(x_ref, xp_ref, sm_ref, z_ref, cw_ref, alog_ref, dtb_ref, ltri_ref, utri_ref, same_ref, tot_ref,
                     q_ref, k_ref, kb_ref, vb_ref, kbe_ref, qg_ref, kg_ref, zs_ref, dec_ref, egl_ref):
    first = pl.program_id(1) == 0
    dh = GDN_HEAD_DIM
    w = MIX_W
    x = x_ref[...].astype(F32)
    ts = x.shape[0]
    prev8 = _prev_rows(xp_ref, first)
    acc = x * cw_ref[GDN_CONV - 1:GDN_CONV, :]
    for d in range(1, GDN_CONV):
        acc = acc + _shift_down(x, prev8, d) * cw_ref[GDN_CONV - 1 - d:GDN_CONV - d, :]
    act = acc * _sigmoid(acc)
    sm = sm_ref[...].astype(F32)
    gs = [-jnp.exp(alog_ref[:, h:h + 1]) * _softplus(sm[:, SM_GDN_ALPHA + h:SM_GDN_ALPHA + h + 1] + dtb_ref[:, h:h + 1])
          for h in range(GDN_HEADS)]
    gc = jnp.concatenate([jnp.broadcast_to(g, (ts, CHUNK)) for g in gs], axis=1)
    gc3 = _split3(gc)
    gamc = _sum01(ltri_ref[...], gc3)
    sufc = _sum01(utri_ref[...], gc3)
    t_in = lax.broadcasted_iota(jnp.int32, gc.shape, 0) % CHUNK
    s_in = lax.broadcasted_iota(jnp.int32, gc.shape, 1) % CHUNK
    gamr = _sum01(same_ref[...], jnp.where(t_in == s_in, gamc, 0.0))
    dec_ref[...] = jnp.exp(jnp.where(t_in >= s_in, gamc - gamr, NEG))
    eglc = jnp.exp(_sum01(tot_ref[...], gc3))[0:egl_ref.shape[0]]
    egl_ref[...] = jnp.concatenate([eglc[:, h * CHUNK:(h + 1) * CHUNK] for h in range(GDN_HEADS)
                                    for _ in range(dh // CHUNK)],
                                   axis=1)
    for h in range(GDN_HEADS):
        ls = slice(h * dh, (h + 1) * dh)
        q = act[:, h * dh:(h + 1) * dh]
        k = act[:, w + h * dh:w + (h + 1) * dh]
        v = act[:, 2 * w + h * dh:2 * w + (h + 1) * dh]
        q = q * lax.rsqrt(jnp.sum(q * q, axis=-1, keepdims=True) + EPS) * (dh ** -0.5)
        k = k * lax.rsqrt(jnp.sum(k * k, axis=-1, keepdims=True) + EPS)
        beta = _sigmoid(sm[:, SM_GDN_BETA + h:SM_GDN_BETA + h + 1])
        eg = jnp.exp(gamc[:, h * CHUNK:h * CHUNK + 1])
        es = jnp.exp(sufc[:, h * CHUNK:h * CHUNK + 1])
        kb = k * beta
        q_ref[:, ls] = q.astype(BF16)
        k_ref[:, ls] = k.astype(BF16)
        kb_ref[:, ls] = kb.astype(BF16)
        vb_ref[:, ls] = (v * beta).astype(BF16)
        kbe_ref[:, ls] = (kb * eg).astype(BF16)
        qg_ref[:, ls] = (q * eg).astype(BF16)
        kg_ref[:, ls] = (k * es).astype(BF16)
    z = z_ref[...].astype(F32)
    zs_ref[...] = (z * _sigmoid(z)).astype(BF16)


def _gdn_prep(proj3, conv_w, a_log, dt_bias):
    b, s, _ = proj3.shape
    w = MIX_W
    ts = PREP_TILE
    nrb = ts // PREV_ROWS
    nck = ts // CHUNK
    cw = HEADS_PER_PASS * CHUNK
    ltri, utri, same, tot = _chunk_consts(ts)
    full = lambda shp: pl.BlockSpec(shp, lambda bi, i: (0,) * len(shp))
    seq = pl.BlockSpec((None, ts, w), lambda bi, i: (bi, i, 0))
    return pl.pallas_call(
        _gdn_prep_kernel,
        grid=(b, s // ts),
        in_specs=[pl.BlockSpec((None, ts, 3 * w), lambda bi, i: (bi, i, C_GDN // (3 * w))),
                  pl.BlockSpec((None, PREV_ROWS, 3 * w),
                               lambda bi, i: (bi, jnp.maximum(i * nrb - 1, 0), C_GDN // (3 * w))),
                  pl.BlockSpec((None, ts, 128), lambda bi, i: (bi, i, C_SM0 // 128)),
                  pl.BlockSpec((None, ts, w), lambda bi, i: (bi, i, (C_GDN + 3 * w) // w)),
                  full((GDN_CONV, 3 * w)), full((1, GDN_HEADS)), full((1, GDN_HEADS)),
                  full((ts, ts)), full((ts, ts)), full((ts, ts)), full(tot.shape)],
        out_specs=[seq] * 8 + [pl.BlockSpec((None, ts, cw), lambda bi, i: (bi, i, 0)),
                               pl.BlockSpec((None, nck, w), lambda bi, i: (bi, i, 0))],
        out_shape=[jax.ShapeDtypeStruct((b, s, w), BF16)] * 8 + [jax.ShapeDtypeStruct((b, s, cw), F32),
                                                                 jax.ShapeDtypeStruct((b, s // CHUNK, w), F32)],
        compiler_params=_cp("parallel", "parallel"),
        name="gdn_prep",
    )(proj3, proj3, proj3, proj3, conv_w, a_log.reshape(1, GDN_HEADS), dt_bias.reshape(1, GDN_HEADS),
      ltri, utri, same, tot)


def _gdn_chunk_kernel(q_ref, k_ref, kb_ref, vb_ref, kbe_ref, qg_ref, kg_ref, zs_ref, dec_ref, egl_ref, nw_ref,
                      o_ref, s_ref):
    c = CHUNK
    dh = GDN_HEAD_DIM
    w = MIX_W
    pw = 2 * dh
    npair = w // pw
    nseq = q_ref.shape[0]

    @pl.when(pl.program_id(1) == 0)
    def _():
        s_ref[...] = jnp.zeros_like(s_ref)

    _, strict, eye = _cat_tri_masks()
    cw = HEADS_PER_PASS * c
    bd64 = jnp.where(_block_mask(cw, cw, c, c), 1.0, 0.0).astype(BF16)
    bdk = jnp.where(_block_mask(cw, w, c, dh), 1.0, 0.0).astype(BF16)
    bdp_b = _block_mask(pw, pw, dh, dh)

    seqs = range(nseq)
    pairs = [slice(p * pw, (p + 1) * pw) for p in range(npair)]

    def chunk(ci, carry):
        rows = pl.ds(pl.multiple_of(ci * c, c), c)
        lhs = [jnp.concatenate([kb_ref[sq, rows, :], q_ref[sq, rows, :]], axis=0) for sq in seqs]
        aq = [_dot_nt(lhs[sq], _bd_tile(k_ref[sq, rows, :], bdk)) for sq in seqs]
        dec = [dec_ref[sq, rows, :] for sq in seqs]
        p = [-jnp.where(strict, aq[sq][0:c] * dec[sq], 0.0) for sq in seqs]
        t = [jnp.where(eye, 1.0, 0.0) + p[sq] for sq in seqs]
        for _ in range(int(math.log2(c)) - 1):
            pb = [x.astype(BF16) for x in p]
            p = [_dot(pb[sq], _bd_tile(pb[sq], bd64)) for sq in seqs]
            t = [t[sq] + _dot(t[sq].astype(BF16), _bd_tile(p[sq].astype(BF16), bd64)) for sq in seqs]
        tb = [x.astype(BF16) for x in t]
        u = [_dot(tb[sq], _bd_tile(vb_ref[sq, rows, :], bdk)) for sq in seqs]
        wm = [_dot(tb[sq], _bd_tile(kbe_ref[sq, rows, :], bdk)).astype(BF16) for sq in seqs]
        st = [[s_ref[sq, pi] for pi in range(npair)] for sq in seqs]
        stb = [[x.astype(BF16) for x in st[sq]] for sq in seqs]
        ws = [jnp.concatenate([_dot(wm[sq][:, ps], stb[sq][pi]) for pi, ps in enumerate(pairs)], axis=1)
              for sq in seqs]
        vnb = [(u[sq] - ws[sq]).astype(BF16) for sq in seqs]
        qs = [jnp.concatenate([_dot(qg_ref[sq, rows, ps], stb[sq][pi]) for pi, ps in enumerate(pairs)], axis=1)
              for sq in seqs]
        o = [qs[sq] + _dot((aq[sq][c:2 * c] * dec[sq]).astype(BF16), _bd_tile(vnb[sq], bdk)) for sq in seqs]
        upd = [[_dot_tn(kg_ref[sq, rows, ps], vnb[sq][:, ps]) for ps in pairs] for sq in seqs]
        for sq in seqs:
            egl = egl_ref[sq, pl.ds(ci, 1), :]
            for pi, ps in enumerate(pairs):
                s_ref[sq, pi] = st[sq][pi] * egl[:, ps] + jnp.where(bdp_b, upd[sq][pi], 0.0)
            for h in range(GDN_HEADS):
                ls = slice(h * dh, (h + 1) * dh)
                oh = o[sq][:, ls]
                ms = jnp.mean(oh * oh, axis=-1, keepdims=True)
                o_ref[sq, rows, ls] = (oh * lax.rsqrt(ms + EPS) * nw_ref[...]
                                       * zs_ref[sq, rows, ls].astype(F32)).astype(o_ref.dtype)
        return carry

    lax.fori_loop(0, q_ref.shape[1] // c, chunk, 0)


def _gdn_chunk(ops, dec, egl, norm_w):
    b, s, w = ops[0].shape
    dh = GDN_HEAD_DIM
    nsq = _seq_per_step(b)
    lblk = REC_BLOCK
    seq = pl.BlockSpec((nsq, lblk, w), lambda bi, i: (bi, i, 0))
    egl, egl_spec = _per_chunk_spec(egl, nsq)
    return pl.pallas_call(
        _gdn_chunk_kernel,
        grid=(b // nsq, s // lblk),
        in_specs=[seq] * 8 + [pl.BlockSpec((nsq, lblk, dec.shape[2]), lambda bi, i: (bi, i, 0)), egl_spec,
                              pl.BlockSpec((1, dh), lambda bi, i: (0, 0))],
        out_specs=seq,
        out_shape=jax.ShapeDtypeStruct((b, s, w), BF16),
        scratch_shapes=[pltpu.VMEM((nsq, w // (2 * dh), 2 * dh, 2 * dh), F32)],
        compiler_params=_cp("parallel", "arbitrary"),
        name="gdn_chunk",
    )(*ops, dec, egl, norm_w.reshape(1, dh))


def _merge_kernel(x_ref, oa_ref, ob_ref, oc_ref, ga_ref, gb_ref, gc_ref, wb_ref, wo_ref, o_ref):
    merged = None
    for j, (br, gate) in enumerate(((oa_ref, ga_ref), (ob_ref, gb_ref), (oc_ref, gc_ref))):
        t = _sigmoid(gate[...].astype(F32)) * _dot(br[...], wb_ref[j])
        merged = t if merged is None else merged + t
    o_ref[...] = x_ref[...] + _dot(merged.astype(BF16), wo_ref[...])


def _merge(x2, o_a, o_b, o_c, proj, w_branch, w_out, tm=512):
    t, d = x2.shape
    w = MIX_W
    tm = min(tm, t)
    br = pl.BlockSpec((tm, w), lambda i: (i, 0))
    gate = lambda j: pl.BlockSpec((tm, d), lambda i: (i, C_GATE // d + j))
    return pl.pallas_call(
        _merge_kernel,
        grid=(t // tm,),
        in_specs=[pl.BlockSpec((tm, d), lambda i: (i, 0)), br, br, br, gate(0), gate(1), gate(2),
                  pl.BlockSpec((3, w, d), lambda i: (0, 0, 0)),
                  pl.BlockSpec((d, d), lambda i: (0, 0))],
        out_specs=pl.BlockSpec((tm, d), lambda i: (i, 0)),
        out_shape=jax.ShapeDtypeStruct((t, d), F32),
        compiler_params=_cp("parallel"),
        name="merge",
    )(x2, o_a.reshape(t, w), o_b.reshape(t, w), o_c.reshape(t, w), proj, proj, proj,
      w_branch.astype(BF16), w_out.astype(BF16))


def _ffn_kernel(x_ref, g_ref, w1_ref, w2_ref, o_ref, hn_ref):
    j = pl.program_id(1)

    @pl.when(j == 0)
    def _():
        x = x_ref[...]
        ms = jnp.mean(x * x, axis=-1, keepdims=True)
        hn_ref[...] = (x * lax.rsqrt(ms + EPS) * g_ref[...]).astype(BF16)
        o_ref[...] = x

    h1 = jnp.maximum(_dot(hn_ref[...], w1_ref[...]), 0.0)
    o_ref[...] += _dot((h1 * h1).astype(BF16), w2_ref[...])


def _ffn(x2, g, w1, w2, tm=1024, tf=1024):
    t, d = x2.shape
    f = w1.shape[1]
    tm = min(tm, t)
    return pl.pallas_call(
        _ffn_kernel,
        grid=(t // tm, f // tf),
        in_specs=[pl.BlockSpec((tm, d), lambda i, j: (i, 0)),
                  pl.BlockSpec((1, d), lambda i, j: (0, 0)),
                  pl.BlockSpec((d, tf), lambda i, j: (0, j)),
                  pl.BlockSpec((tf, d), lambda i, j: (j, 0))],
        out_specs=pl.BlockSpec((tm, d), lambda i, j: (i, 0)),
        out_shape=jax.ShapeDtypeStruct((t, d), F32),
        scratch_shapes=[pltpu.VMEM((tm, d), BF16)],
        compiler_params=_cp("parallel", "arbitrary"),
        name="ffn",
    )(x2, g.reshape(1, d), w1.astype(BF16), w2.astype(BF16))


def kernel(x, norm_mix_g, w_in, nsa_q_norm, nsa_k_norm, nsa_cmp_pos, nsa_cmp_w1, nsa_cmp_w2, rwkv_mu, rwkv_w0, rwkv_w_up, rwkv_a0, rwkv_a_up, rwkv_g_up, rwkv_k_k, rwkv_k_a, rwkv_r_k, rwkv_ln_w, rwkv_ln_b, rwkv_v0, rwkv_vres_down, rwkv_vres_up, gdn_conv_w, gdn_a_log, gdn_dt_bias, gdn_norm_w, w_branch, w_out, norm_ffn_g, w_ff1, w_ff2):
    b, s, d = x.shape
    depth = w_in.shape[0]
    perm = jnp.asarray(_proj_perm())
    x2 = x.reshape(b * s, d)
    v_first = None
    for i in range(depth):
        vd = rwkv_vres_down[i - 1] if i > 0 else jnp.zeros((d, VRES_RANK), F32)
        w_ext = jnp.concatenate([w_in[i], vd, jnp.zeros((d, 1), F32)], axis=1)
        w_all = jnp.take(w_ext, perm, axis=1).astype(BF16)
        proj = _rms_matmul(x2, norm_mix_g[i], w_all)
        proj3 = proj.reshape(b, s, N_PROJ)

        o_a = _nsa(proj3, nsa_q_norm[i], nsa_k_norm[i], nsa_cmp_pos[i], nsa_cmp_w1[i], nsa_cmp_w2[i])

        vres = None if i == 0 else (rwkv_v0[i - 1], rwkv_vres_up[i - 1])
        rw = _rwkv_prep(proj3, rwkv_mu[i], rwkv_w0[i], rwkv_w_up[i], rwkv_a0[i], rwkv_a_up[i], rwkv_g_up[i],
                        rwkv_k_k[i], rwkv_k_a[i], rwkv_r_k[i], v_first, vres)
        if i == 0:
            v_first = rw[10]
        o_b = _rwkv_chunk(rw[:9], rw[9], rwkv_ln_w[i], rwkv_ln_b[i])

        gd = _gdn_prep(proj3, gdn_conv_w[i], gdn_a_log[i], gdn_dt_bias[i])
        o_c = _gdn_chunk(gd[:8], gd[8], gd[9], gdn_norm_w[i])

        x2 = _merge(x2, o_a, o_b, o_c, proj, w_branch[i], w_out[i])
        x2 = _ffn(x2, norm_ffn_g[i], w_ff1[i], w_ff2[i])
    return x2.reshape(b, s, d)
```
